```python
import functools
import jax, jax.numpy as jnp
from jax import lax
import numpy as np

D_MODEL = 2048
BATCH = 4
SEQ = 2048
DEPTH = 2
DEC_BATCH = 128
DEC_SEQ = 8
PAST_LEN = 8192
PAGE_SIZE = 128

D_MIX = D_MODEL
LRU_WIDTH = D_MIX // 4
MLA_WIDTH = D_MIX // 2
CONV_WIDTH = D_MIX - LRU_WIDTH - MLA_WIDTH
LRU_HEADS = 8
LRU_HEAD_DIM = LRU_WIDTH // LRU_HEADS
LRU_CONV = 4
LRU_C = 8.0
MLA_HEADS = 8
V_HEAD_DIM = MLA_WIDTH // MLA_HEADS
QK_NOPE = 128
QK_ROPE = 64
Q_RANK = D_MODEL // 4
KV_RANK = D_MODEL // 4
ROPE_THETA = 10000.0
ATTN_SCALE = (QK_NOPE + QK_ROPE) ** -0.5
Q_BLOCK = 128
CONV_K = 31
RMS_EPS = 1e-6
LN_EPS = 1e-5
COL_SIZES = (LRU_WIDTH, LRU_WIDTH,
             Q_RANK, KV_RANK, QK_ROPE, MLA_WIDTH,
             CONV_WIDTH, CONV_WIDTH, CONV_WIDTH)
D_IN = sum(COL_SIZES)

kernel_name = 'hymba_lru_mla_conformer_decode_step'


def rms_norm(x, g):
    xf = x.astype(jnp.float32)
    y = xf * lax.rsqrt(jnp.mean(xf * xf, axis=-1, keepdims=True) + RMS_EPS)
    return (y * g.astype(jnp.float32)).astype(x.dtype)


def layer_norm(x, g, b):
    xf = x.astype(jnp.float32)
    mu = jnp.mean(xf, axis=-1, keepdims=True)
    xc = xf - mu
    y = xc * lax.rsqrt(jnp.mean(xc * xc, axis=-1, keepdims=True) + LN_EPS)
    return (y * g.astype(jnp.float32) + b.astype(jnp.float32)).astype(x.dtype)


def rope(x, pos):
    half = QK_ROPE // 2
    freqs = ROPE_THETA ** (-jnp.arange(half, dtype=jnp.float32) / half)
    ang = pos.astype(jnp.float32)[:, None] * freqs[None, :]
    shape = (1, ang.shape[0]) + (1,) * (x.ndim - 3) + (half,)
    cos, sin = jnp.cos(ang).reshape(shape), jnp.sin(ang).reshape(shape)
    xf = x.astype(jnp.float32)
    x1, x2 = xf[..., :half], xf[..., half:]
    return jnp.concatenate([x1 * cos - x2 * sin, x1 * sin + x2 * cos], axis=-1).astype(x.dtype)


def split_cols(z):
    offs, o = [], 0
    for s in COL_SIZES[:-1]:
        o += s
        offs.append(o)
    return jnp.split(z, offs, axis=-1)


def causal_dwconv(x, buf, w, b):
    xp = jnp.concatenate([buf.astype(x.dtype), x], axis=1)
    y = lax.conv_general_dilated(xp, w[:, None, :].astype(x.dtype), window_strides=(1,), padding='VALID',
                                 dimension_numbers=('NWC', 'WIO', 'NWC'), feature_group_count=x.shape[-1])
    return y + b.astype(x.dtype), xp[:, xp.shape[1] - (w.shape[0] - 1):]


def block_diag(x, w):
    b, l, _ = x.shape
    xr = x.reshape(b, l, LRU_HEADS, LRU_HEAD_DIM)
    return jnp.einsum('blnw,nwv->blnv', xr, w).reshape(b, l, LRU_WIDTH)


def rg_lru(x, h0, w_a, b_a, w_x, b_x, lam):
    r = jax.nn.sigmoid((block_diag(x, w_a) + b_a).astype(jnp.float32))
    i = jax.nn.sigmoid((block_diag(x, w_x) + b_x).astype(jnp.float32))
    log_a = -LRU_C * r * jax.nn.softplus(-lam.astype(jnp.float32))
    a = jnp.exp(log_a)
    u = jnp.sqrt(-jnp.expm1(2.0 * log_a)) * i * x.astype(jnp.float32)

    def step(h, au):
        a_t, u_t = au
        h = a_t * h + u_t
        return h, h

    h_last, hs = lax.scan(step, h0.astype(jnp.float32), (a.swapaxes(0, 1), u.swapaxes(0, 1)))
    return hs.swapaxes(0, 1).astype(x.dtype), h_last.astype(x.dtype)


def conformer_conv(v, g, buf, w_dw, b_dw, ln_g, ln_b):
    u = v * jax.nn.sigmoid(g)
    y, new_buf = causal_dwconv(u, buf, w_dw, b_dw)
    y = layer_norm(y, ln_g, ln_b)
    return jax.nn.silu(y), new_buf


def mla_project(q_lat, kv_lat, k_r, pos, q_norm, kv_norm, w_uq, w_uk):
    cq = rms_norm(q_lat, q_norm)
    q = jnp.einsum('blr,rhd->blhd', cq, w_uq)
    q_rope = rope(q[..., QK_NOPE:], pos)
    q_abs = jnp.einsum('blhd,rhd->blhr', q[..., :QK_NOPE], w_uk)
    c_kv = rms_norm(kv_lat, kv_norm)
    k_rope = rope(k_r, pos)
    return q_abs, q_rope, c_kv, k_rope


def mla_scores(q_abs, q_rope, c_kv, k_rope):
    s = jnp.einsum('bqhr,bkr->bhqk', q_abs, c_kv) + jnp.einsum('bqhd,bkd->bhqk', q_rope, k_rope)
    return s.astype(jnp.float32) * ATTN_SCALE


def prompt_attention(q_abs, q_rope, c_kv, k_rope):
    b, l, h, r = q_abs.shape
    nb = l // Q_BLOCK
    qa = q_abs.reshape(b, nb, Q_BLOCK, h, r).swapaxes(0, 1)
    qr = q_rope.reshape(b, nb, Q_BLOCK, h, QK_ROPE).swapaxes(0, 1)
    k_pos = jnp.arange(l)
    c_kv32 = c_kv.astype(jnp.float32)

    def one_block(args):
        qa_b, qr_b, blk = args
        s = mla_scores(qa_b, qr_b, c_kv, k_rope)
        q_pos = blk * Q_BLOCK + jnp.arange(Q_BLOCK)
        s = jnp.where(k_pos[None, :] <= q_pos[:, None], s, -jnp.inf)
        p = jax.nn.softmax(s, axis=-1)
        return jnp.einsum('bhqk,bkr->bqhr', p, c_kv32)

    o = lax.map(one_block, (qa, qr, jnp.arange(nb)))
    return o.swapaxes(0, 1).reshape(b, l, h, r)


def sample_attention(q_abs, q_rope, c_kv, k_rope, cache_lat, cache_rope, page_table):
    t = q_abs.shape[1]
    s = mla_scores(q_abs, q_rope, c_kv, k_rope)
    s = jnp.where(jnp.tril(jnp.ones((t, t), dtype=bool)), s, -jnp.inf)
    m = jnp.max(s, axis=-1)
    p = jnp.exp(s - m[..., None])
    l = jnp.sum(p, axis=-1)
    acc = jnp.einsum('bhqk,bkr->bhqr', p, c_kv.astype(jnp.float32))

    def page_step(carry, pages):
        m, l, acc = carry
        lat = cache_lat[pages]
        rop = cache_rope[pages]
        s = mla_scores(q_abs, q_rope, lat, rop)
        m_new = jnp.maximum(m, jnp.max(s, axis=-1))
        corr = jnp.exp(m - m_new)
        p = jnp.exp(s - m_new[..., None])
        l = l * corr + jnp.sum(p, axis=-1)
        acc = acc * corr[..., None] + jnp.einsum('bhqk,bkr->bhqr', p, lat.astype(jnp.float32))
        return (m_new, l, acc), None

    (m, l, acc), _ = lax.scan(page_step, (m, l, acc), page_table.T)
    return (acc / l[..., None]).transpose(0, 2, 1, 3)


def mixer_layer(x, pos, lru_h0, lru_buf, conv_buf, attend, p):
    b, l, _ = x.shape
    hn = rms_norm(x, p['norm_g'])
    z = hn @ p['w_in']
    lru_x, lru_g, q_lat, kv_lat, k_r, mla_g, conv_v, conv_glu, conv_g = split_cols(z)
    xc, lru_buf_new = causal_dwconv(lru_x, lru_buf, p['lru_conv_w'], p['lru_conv_b'])
    hs, h_last = rg_lru(xc, lru_h0, p['lru_w_a'], p['lru_b_a'], p['lru_w_x'], p['lru_b_x'], p['lru_lambda'])
    out_a = hs * jax.nn.silu(lru_g)
    q_abs, q_rope, c_kv, k_rope = mla_project(q_lat, kv_lat, k_r, pos, p['mla_q_norm'], p['mla_kv_norm'],
                                              p['mla_w_uq'], p['mla_w_uk'])
    o_lat = attend(q_abs, q_rope, c_kv, k_rope).astype(x.dtype)
    out_b = jnp.einsum('blhr,rhd->blhd', o_lat, p['mla_w_uv']).reshape(b, l, MLA_WIDTH) * jax.nn.silu(mla_g)
    yc, conv_buf_new = conformer_conv(conv_v, conv_glu, conv_buf, p['conv_dw_w'], p['conv_dw_b'],
                                      p['conv_ln_g'], p['conv_ln_b'])
    out_c = yc * jax.nn.silu(conv_g)
    mix = jnp.concatenate([out_a, out_b, out_c], axis=-1)
    y = x + mix @ p['w_out']
    return y, (c_kv, k_rope, h_last, lru_buf_new, conv_buf_new)


def setup_inputs(seed: int = 0) -> dict:
    key = jax.random.key(seed)
    ks = jax.random.split(key, 32)
    f32 = jnp.float32

    def nrm(k, shape, scale):
        return jax.random.normal(k, shape, f32) * scale

    n_pages = PAST_LEN // PAGE_SIZE
    n_used = DEC_BATCH * n_pages
    n_phys = n_used + max(1, n_used // 4)
    page_table = jax.random.permutation(ks[0], n_phys)[:n_used].reshape(DEC_BATCH, n_pages).astype(jnp.int32)
    u = jax.random.uniform(ks[1], (DEPTH, LRU_WIDTH), f32, minval=0.9, maxval=0.999)
    s = u ** (1.0 / LRU_C)
    lru_lambda = jnp.log(s) - jnp.log1p(-s)
    return {
        'x_prompt': nrm(ks[2], (BATCH, SEQ, D_MODEL), 1.0),
        'x_sample': nrm(ks[3], (DEC_BATCH, DEC_SEQ, D_MODEL), 1.0),
        'cache_kv_latent': nrm(ks[4], (DEPTH, n_phys, PAGE_SIZE, KV_RANK), 1.0),
        'cache_k_rope': nrm(ks[5], (DEPTH, n_phys, PAGE_SIZE, QK_ROPE), 1.0),
        'page_table': page_table,
        'state_lru_h': nrm(ks[6], (DEPTH, DEC_BATCH, LRU_WIDTH), 0.5),
        'state_lru_conv': nrm(ks[7], (DEPTH, DEC_BATCH, LRU_CONV - 1, LRU_WIDTH), 1.0),
        'state_conv': nrm(ks[8], (DEPTH, DEC_BATCH, CONV_K - 1, CONV_WIDTH), 0.5),
        'norm_g': 1.0 + nrm(ks[9], (DEPTH, D_MODEL), 0.02),
        'w_in': nrm(ks[10], (DEPTH, D_MODEL, D_IN), D_MODEL ** -0.5),
        'w_out': nrm(ks[11], (DEPTH, D_MIX, D_MODEL), D_MIX ** -0.5),
        'lru_conv_w': nrm(ks[12], (DEPTH, LRU_CONV, LRU_WIDTH), LRU_CONV ** -0.5),
        'lru_conv_b': nrm(ks[13], (DEPTH, LRU_WIDTH), 0.01),
        'lru_w_a': nrm(ks[14], (DEPTH, LRU_HEADS, LRU_HEAD_DIM, LRU_HEAD_DIM), LRU_HEAD_DIM ** -0.5),
        'lru_b_a': nrm(ks[15], (DEPTH, LRU_WIDTH), 0.01),
        'lru_w_x': nrm(ks[16], (DEPTH, LRU_HEADS, LRU_HEAD_DIM, LRU_HEAD_DIM), LRU_HEAD_DIM ** -0.5),
        'lru_b_x': nrm(ks[17], (DEPTH, LRU_WIDTH), 0.01),
        'lru_lambda': lru_lambda,
        'mla_q_norm': 1.0 + nrm(ks[18], (DEPTH, Q_RANK), 0.02),
        'mla_kv_norm': 1.0 + nrm(ks[19], (DEPTH, KV_RANK), 0.02),
        'mla_w_uq': nrm(ks[20], (DEPTH, Q_RANK, MLA_HEADS, QK_NOPE + QK_ROPE), Q_RANK ** -0.5),
        'mla_w_uk': nrm(ks[21], (DEPTH, KV_RANK, MLA_HEADS, QK_NOPE), KV_RANK ** -0.5),
        'mla_w_uv': nrm(ks[22], (DEPTH, KV_RANK, MLA_HEADS, V_HEAD_DIM), KV_RANK ** -0.5),
        'conv_dw_w': nrm(ks[23], (DEPTH, CONV_K, CONV_WIDTH), CONV_K ** -0.5),
        'conv_dw_b': nrm(ks[24], (DEPTH, CONV_WIDTH), 0.01),
        'conv_ln_g': 1.0 + nrm(ks[25], (DEPTH, CONV_WIDTH), 0.02),
        'conv_ln_b': nrm(ks[26], (DEPTH, CONV_WIDTH), 0.01),
        'final_norm_g': 1.0 + nrm(ks[27], (D_MODEL,), 0.02),
    }


def reference(x_prompt, x_sample, cache_kv_latent, cache_k_rope, page_table, state_lru_h, state_lru_conv,
              state_conv, norm_g, w_in, w_out, lru_conv_w, lru_conv_b, lru_w_a, lru_b_a, lru_w_x, lru_b_x,
              lru_lambda, mla_q_norm, mla_kv_norm, mla_w_uq, mla_w_uk, mla_w_uv, conv_dw_w, conv_dw_b,
              conv_ln_g, conv_ln_b, final_norm_g):
    b_p, seq = x_prompt.shape[0], x_prompt.shape[1]
    dec_seq = x_sample.shape[1]
    past_len = page_table.shape[1] * PAGE_SIZE
    pos_p = jnp.arange(seq, dtype=jnp.float32)
    pos_s = past_len + jnp.arange(dec_seq, dtype=jnp.float32)
    dt = x_prompt.dtype
    h0_p = jnp.zeros((b_p, LRU_WIDTH), dt)
    lbuf_p = jnp.zeros((b_p, LRU_CONV - 1, LRU_WIDTH), dt)
    cbuf_p = jnp.zeros((b_p, CONV_K - 1, CONV_WIDTH), dt)
    xp, xs = x_prompt, x_sample
    st_p, st_s = [], []
    for l in range(DEPTH):
        p = {'norm_g': norm_g[l], 'w_in': w_in[l], 'w_out': w_out[l],
             'lru_conv_w': lru_conv_w[l], 'lru_conv_b': lru_conv_b[l],
             'lru_w_a': lru_w_a[l], 'lru_b_a': lru_b_a[l], 'lru_w_x': lru_w_x[l], 'lru_b_x': lru_b_x[l],
             'lru_lambda': lru_lambda[l], 'mla_q_norm': mla_q_norm[l], 'mla_kv_norm': mla_kv_norm[l],
             'mla_w_uq': mla_w_uq[l], 'mla_w_uk': mla_w_uk[l], 'mla_w_uv': mla_w_uv[l],
             'conv_dw_w': conv_dw_w[l], 'conv_dw_b': conv_dw_b[l],
             'conv_ln_g': conv_ln_g[l], 'conv_ln_b': conv_ln_b[l]}
        xp, sp = mixer_layer(xp, pos_p, h0_p, lbuf_p, cbuf_p, prompt_attention, p)
        attend_s = functools.partial(sample_attention, cache_lat=cache_kv_latent[l],
                                     cache_rope=cache_k_rope[l], page_table=page_table)
        xs, ss = mixer_layer(xs, pos_s, state_lru_h[l], state_lru_conv[l], state_conv[l], attend_s, p)
        st_p.append(sp)
        st_s.append(ss)
    y_prompt = rms_norm(xp, final_norm_g)
    y_sample = rms_norm(xs, final_norm_g)
    new_kv_latent_prompt = jnp.stack([s[0] for s in st_p])
    new_k_rope_prompt = jnp.stack([s[1] for s in st_p])
    new_lru_h_prompt = jnp.stack([s[2] for s in st_p])
    new_lru_conv_prompt = jnp.stack([s[3] for s in st_p])
    new_conv_prompt = jnp.stack([s[4] for s in st_p])
    new_kv_latent_sample = jnp.stack([s[0] for s in st_s])
    new_k_rope_sample = jnp.stack([s[1] for s in st_s])
    new_lru_h_sample = jnp.stack([s[2] for s in st_s])
    new_lru_conv_sample = jnp.stack([s[3] for s in st_s])
    new_conv_sample = jnp.stack([s[4] for s in st_s])
    return (y_prompt, y_sample, new_kv_latent_prompt, new_k_rope_prompt, new_lru_h_prompt,
            new_lru_conv_prompt, new_conv_prompt, new_kv_latent_sample, new_k_rope_sample,
            new_lru_h_sample, new_lru_conv_sample, new_conv_sample)
```

```python
import functools

import jax
import jax.numpy as jnp
from jax import lax
from jax.experimental import pallas as pl
from jax.experimental.pallas import tpu as pltpu

F32 = jnp.float32
BF16 = jnp.bfloat16

LRU_HEADS = 8
LRU_CONV = 4
LRU_C = 8.0
MLA_HEADS = 8
QK_NOPE = 128
QK_ROPE = 64
ROPE_THETA = 10000.0
ATTN_SCALE = (QK_NOPE + QK_ROPE) ** -0.5
PAGE_SIZE = 128
CONV_K = 31
RMS_EPS = 1e-6
LN_EPS = 1e-5

LANES = 128
SUBLANES = 8
VMEM_LIMIT_BYTES = 56 * 1024 * 1024

ROW_TILE = 512
IN_COL_TILE = 1536
SEQ_TILE = 256
Q_TILE = 128
KV_TILE = 512
PAGES_PER_STEP = 8


def _params(*sem):
    return pltpu.CompilerParams(dimension_semantics=sem, vmem_limit_bytes=VMEM_LIMIT_BYTES)


def _dot(a, b):
    return jnp.dot(a, b, preferred_element_type=F32)


def _dot_nt(a, b):
    return lax.dot_general(a, b, (((1,), (1,)), ((), ())), preferred_element_type=F32)


def _rms(x, g):
    return x * lax.rsqrt(jnp.mean(x * x, axis=-1, keepdims=True) + RMS_EPS) * g


def _silu(x):
    return x * jax.nn.sigmoid(x)


def _in_proj_kernel(x_ref, g_ref, w_ref, wkr_ref, z_ref, zkr_ref, hn_ref):
    @pl.when(pl.program_id(1) == 0)
    def _():
        hn = _rms(x_ref[...], g_ref[...]).astype(BF16)
        hn_ref[...] = hn
        zkr_ref[...] = _dot(hn, wkr_ref[...])

    z_ref[...] = _dot(hn_ref[...], w_ref[...])


def _in_proj(x, g, w_main, w_kr):
    n, d = x.shape
    cols = w_main.shape[1]
    tm = min(ROW_TILE, n)
    tn = IN_COL_TILE
    return pl.pallas_call(
        _in_proj_kernel,
        grid=(n // tm, cols // tn),
        in_specs=[
            pl.BlockSpec((tm, d), lambda i, j: (i, 0)),
            pl.BlockSpec((1, d), lambda i, j: (0, 0)),
            pl.BlockSpec((d, tn), lambda i, j: (0, j)),
            pl.BlockSpec((d, LANES), lambda i, j: (0, 0)),
        ],
        out_specs=[
            pl.BlockSpec((tm, tn), lambda i, j: (i, j)),
            pl.BlockSpec((tm, LANES), lambda i, j: (i, 0)),
        ],
        out_shape=[jax.ShapeDtypeStruct((n, cols), F32), jax.ShapeDtypeStruct((n, LANES), F32)],
        scratch_shapes=[pltpu.VMEM((tm, d), BF16)],
        compiler_params=_params("parallel", "arbitrary"),
        name="in_proj",
    )(x, g, w_main, w_kr)


def _lru_kernel(zx_ref, zg_ref, buf_ref, h0_ref, cw_ref, cb_ref, wa_ref, ba_ref, wx_ref, bx_ref, lam_ref,
                out_ref, hlast_ref, nbuf_ref, xp_ref, h_ref, a_ref, u_ref, hs_ref, *, tl):
    pad = SUBLANES - (LRU_CONV - 1)

    @pl.when(pl.program_id(1) == 0)
    def _():
        xp_ref[pad:SUBLANES, :] = buf_ref[...]
        h_ref[...] = h0_ref[...]

    x = zx_ref[...]
    xp_ref[SUBLANES:SUBLANES + tl, :] = x
    xc = cb_ref[...] + cw_ref[LRU_CONV - 1:LRU_CONV, :] * x
    for k in range(LRU_CONV - 1):
        xc = xc + cw_ref[k:k + 1, :] * xp_ref[pad + k:pad + k + tl, :]
    tail = xp_ref[pad + tl:SUBLANES + tl, :]
    xp_ref[pad:SUBLANES, :] = tail
    nbuf_ref[...] = tail

    xcb = xc.astype(BF16)
    r = jax.nn.sigmoid(_dot(xcb, wa_ref[...]) + ba_ref[...])
    i = jax.nn.sigmoid(_dot(xcb, wx_ref[...]) + bx_ref[...])
    softplus_neg_lam = jnp.log(1.0 + jnp.exp(-lam_ref[...]))
    log_a = -LRU_C * r * softplus_neg_lam
    a = jnp.exp(log_a)
    a_ref[...] = a
    u_ref[...] = jnp.sqrt(1.0 - a * a) * i * xc

    def step(t, h):
        h = a_ref[pl.ds(t, 1), :] * h + u_ref[pl.ds(t, 1), :]
        hs_ref[pl.ds(t, 1), :] = h
        return h

    h = lax.fori_loop(0, tl, step, h_ref[...], unroll=SUBLANES)
    h_ref[...] = h
    hlast_ref[...] = h
    out_ref[...] = hs_ref[...] * _silu(zg_ref[...])


def _lru(z, layer, buf, h0, cw, cb, wa, ba, wx, bx, lam, batch, seq):
    w = cw.shape[1]
    tl = min(SEQ_TILE, seq)
    nc = seq // tl
    row = lambda b, c: (b * nc + c, 0)
    vec = pl.BlockSpec((1, w), lambda b, c: (0, 0))
    mat = pl.BlockSpec((w, w), lambda b, c: (0, 0))
    return pl.pallas_call(
        functools.partial(_lru_kernel, tl=tl),
        grid=(batch, nc),
        in_specs=[
            pl.BlockSpec((tl, w), row),
            pl.BlockSpec((tl, w), lambda b, c: (b * nc + c, 1)),
            pl.BlockSpec((None, None, LRU_CONV - 1, w), lambda b, c: (layer, b, 0, 0)),
            pl.BlockSpec((None, None, 1, w), lambda b, c: (layer, b, 0, 0)),
            pl.BlockSpec((LRU_CONV, w), lambda b, c: (0, 0)),
            vec, mat, vec, mat, vec, vec,
        ],
        out_specs=[
            pl.BlockSpec((tl, w), row),
            pl.BlockSpec((None, 1, w), lambda b, c: (b, 0, 0)),
            pl.BlockSpec((None, LRU_CONV - 1, w), lambda b, c: (b, 0, 0)),
        ],
        out_shape=[
            jax.ShapeDtypeStruct((batch * seq, w), F32),
            jax.ShapeDtypeStruct((batch, 1, w), F32),
            jax.ShapeDtypeStruct((batch, LRU_CONV - 1, w), F32),
        ],
        scratch_shapes=[
            pltpu.VMEM((tl + SUBLANES, w), F32),
            pltpu.VMEM((1, w), F32),
            pltpu.VMEM((tl, w), F32),
            pltpu.VMEM((tl, w), F32),
            pltpu.VMEM((tl, w), F32),
        ],
        compiler_params=_params("parallel", "arbitrary"),
        name="lru",
    )(z, z, buf, h0, cw, cb, wa, ba, wx, bx, lam)


CONV_HIST = CONV_K - 1
CONV_PAD = 32


def _conv_kernel(zv_ref, zglu_ref, zg_ref, buf_ref, w_ref, b_ref, lng_ref, lnb_ref,
                 out_ref, nbuf_ref, s_ref, *, tl):
    base = CONV_PAD - CONV_HIST

    @pl.when(pl.program_id(1) == 0)
    def _():
        s_ref[base:CONV_PAD, :] = buf_ref[...]

    s_ref[CONV_PAD:CONV_PAD + tl, :] = zv_ref[...] * jax.nn.sigmoid(zglu_ref[...])
    y = b_ref[...] + w_ref[0:1, :] * s_ref[base:base + tl, :]
    for k in range(1, CONV_K):
        y = y + w_ref[k:k + 1, :] * s_ref[base + k:base + k + tl, :]
    tail = s_ref[base + tl:CONV_PAD + tl, :]
    s_ref[base:CONV_PAD, :] = tail
    nbuf_ref[...] = tail

    mu = jnp.mean(y, axis=-1, keepdims=True)
    yc = y - mu
    yn = yc * lax.rsqrt(jnp.mean(yc * yc, axis=-1, keepdims=True) + LN_EPS) * lng_ref[...] + lnb_ref[...]
    out_ref[...] = _silu(yn) * _silu(zg_ref[...])


def _conv(z, layer, buf, w, b, lng, lnb, batch, seq, col0):
    cw = w.shape[1]
    tl = min(SEQ_TILE, seq)
    nc = seq // tl
    vec = pl.BlockSpec((1, cw), lambda b_, c: (0, 0))
    zspec = lambda off: pl.BlockSpec((tl, cw), lambda b_, c: (b_ * nc + c, col0 + off))
    return pl.pallas_call(
        functools.partial(_conv_kernel, tl=tl),
        grid=(batch, nc),
        in_specs=[
            zspec(0), zspec(1), zspec(2),
            pl.BlockSpec((None, None, CONV_HIST, cw), lambda b_, c: (layer, b_, 0, 0)),
            pl.BlockSpec((CONV_K, cw), lambda b_, c: (0, 0)),
            vec, vec, vec,
        ],
        out_specs=[
            pl.BlockSpec((tl, cw), lambda b_, c: (b_ * nc + c, 0)),
            pl.BlockSpec((None, CONV_HIST, cw), lambda b_, c: (b_, 0, 0)),
        ],
        out_shape=[
            jax.ShapeDtypeStruct((batch * seq, cw), F32),
            jax.ShapeDtypeStruct((batch, CONV_HIST, cw), F32),
        ],
        scratch_shapes=[pltpu.VMEM((tl + CONV_PAD, cw), F32)],
        compiler_params=_params("parallel", "arbitrary"),
        name="conv",
    )(z, z, z, buf, w, b, lng, lnb)


def _rope128(v, table):
    a = v * table
    return a + pltpu.roll(a, QK_ROPE, 1)


def _mla_proj_kernel(zq_ref, zkv_ref, zkr_ref, qn_ref, kvn_ref, wn_ref, wr_ref, wuk_ref, tab_ref,
                     qa_ref, qr_ref, ckv_ref, ckvb_ref, krope_ref, krp_ref):
    tab = tab_ref[...]
    cq = _rms(zq_ref[...], qn_ref[...]).astype(BF16)
    q_nope = _dot(cq, wn_ref[...])
    q_rope = _dot(cq, wr_ref[...])
    for h in range(MLA_HEADS):
        qh = q_nope[:, h * QK_NOPE:(h + 1) * QK_NOPE].astype(BF16)
        qa_ref[h] = _dot(qh, wuk_ref[h]).astype(qa_ref.dtype)
        qr_ref[h] = _rope128(q_rope[:, h * LANES:(h + 1) * LANES], tab).astype(qr_ref.dtype)
    ckv = _rms(zkv_ref[...], kvn_ref[...])
    ckv_ref[...] = ckv
    ckvb_ref[...] = ckv.astype(BF16)
    kro = _rope128(zkr_ref[...], tab)
    krope_ref[...] = kro[:, :QK_ROPE]
    lane = lax.broadcasted_iota(jnp.int32, kro.shape, 1)
    krp_ref[...] = jnp.where(lane < QK_ROPE, kro, 0.0).astype(BF16)


def _mla_proj(z, zkr, qn, kvn, wn, wr, wuk, table, q_dtype):
    n = z.shape[0]
    r = qn.shape[1]
    tm = min(ROW_TILE, n)
    nt = table.shape[0] // tm
    full = lambda shape: pl.BlockSpec(shape, lambda i: (0,) * len(shape))
    return pl.pallas_call(
        _mla_proj_kernel,
        grid=(n // tm,),
        in_specs=[
            pl.BlockSpec((tm, r), lambda i: (i, 2)),
            pl.BlockSpec((tm, r), lambda i: (i, 3)),
            pl.BlockSpec((tm, LANES), lambda i: (i, 0)),
            full((1, r)), full((1, r)),
            full(wn.shape), full(wr.shape), full(wuk.shape),
            pl.BlockSpec((tm, LANES), lambda i: (i % nt, 0)),
        ],
        out_specs=[
            pl.BlockSpec((MLA_HEADS, tm, r), lambda i: (0, i, 0)),
            pl.BlockSpec((MLA_HEADS, tm, LANES), lambda i: (0, i, 0)),
            pl.BlockSpec((tm, r), lambda i: (i, 0)),
            pl.BlockSpec((tm, r), lambda i: (i, 0)),
            pl.BlockSpec((tm, QK_ROPE), lambda i: (i, 0)),
            pl.BlockSpec((tm, LANES), lambda i: (i, 0)),
        ],
        out_shape=[
            jax.ShapeDtypeStruct((MLA_HEADS, n, r), q_dtype),
            jax.ShapeDtypeStruct((MLA_HEADS, n, LANES), q_dtype),
            jax.ShapeDtypeStruct((n, r), F32),
            jax.ShapeDtypeStruct((n, r), BF16),
            jax.ShapeDtypeStruct((n, QK_ROPE), F32),
            jax.ShapeDtypeStruct((n, LANES), BF16),
        ],
        compiler_params=_params("parallel"),
        name="mla_proj",
    )(z, z, zkr, qn, kvn, wn, wr, wuk, table)


def _online_softmax_update(s, v, m_ref, l_ref, acc_ref):
    m_old = m_ref[...]
    m_new = jnp.maximum(m_old, jnp.max(s, axis=-1, keepdims=True))
    corr = jnp.exp(m_old - m_new)
    p = jnp.exp(s - m_new)
    l_ref[...] = l_ref[...] * corr + jnp.sum(p, axis=-1, keepdims=True)
    acc_ref[...] = acc_ref[...] * corr + _dot(p.astype(BF16), v)
    m_ref[...] = m_new


def _head_outputs(o, wuv_ref, rows):
    ob = o.astype(BF16)
    return jnp.concatenate(
        [_dot(ob[h * rows:(h + 1) * rows], wuv_ref[h]) for h in range(MLA_HEADS)], axis=-1)


def _prompt_attn_kernel(qa_ref, qr_ref, kc_ref, kr_ref, wuv_ref, zg_ref, out_ref, m_ref, l_ref, acc_ref,
                        *, tq, tk):
    i = pl.program_id(1)
    j = pl.program_id(2)
    last = (i * tq + tq - 1) // tk
    rows = MLA_HEADS * tq

    @pl.when(j == 0)
    def _():
        m_ref[...] = jnp.full(m_ref.shape, -jnp.inf, F32)
        l_ref[...] = jnp.zeros(l_ref.shape, F32)
        acc_ref[...] = jnp.zeros(acc_ref.shape, F32)

    @pl.when(j <= last)
    def _():
        qa = qa_ref[...].reshape(rows, qa_ref.shape[-1])
        qr = qr_ref[...].reshape(rows, LANES)
        s = (_dot_nt(qa, kc_ref[...]) + _dot_nt(qr, kr_ref[...])) * ATTN_SCALE
        q_pos = i * tq + (lax.broadcasted_iota(jnp.int32, (rows, 1), 0) & (tq - 1))
        k_pos = j * tk + lax.broadcasted_iota(jnp.int32, (1, tk), 1)
        s = jnp.where(k_pos <= q_pos, s, -jnp.inf)
        _online_softmax_update(s, kc_ref[...], m_ref, l_ref, acc_ref)

    @pl.when(j == last)
    def _():
        o = acc_ref[...] / l_ref[...]
        out_ref[...] = _head_outputs(o, wuv_ref, tq) * _silu(zg_ref[...])


def _prompt_attention(qa, qr, kc, kr, wuv, z, batch, seq):
    r = qa.shape[-1]
    tq = min(Q_TILE, seq)
    tk = min(KV_TILE, seq)
    assert tq & (tq - 1) == 0
    nq, nk = seq // tq, seq // tk
    width = wuv.shape[0] * wuv.shape[2]
    kv_row = lambda b, i, j: (b * nk + jnp.minimum(j, (i * tq + tq - 1) // tk), 0)
    return pl.pallas_call(
        functools.partial(_prompt_attn_kernel, tq=tq, tk=tk),
        grid=(batch, nq, nk),
        in_specs=[
            pl.BlockSpec((MLA_HEADS, tq, r), lambda b, i, j: (0, b * nq + i, 0)),
            pl.BlockSpec((MLA_HEADS, tq, LANES), lambda b, i, j: (0, b * nq + i, 0)),
            pl.BlockSpec((tk, r), kv_row),
            pl.BlockSpec((tk, LANES), kv_row),
            pl.BlockSpec(wuv.shape, lambda b, i, j: (0, 0, 0)),
            pl.BlockSpec((tq, width), lambda b, i, j: (b * nq + i, 2)),
        ],
        out_specs=pl.BlockSpec((tq, width), lambda b, i, j: (b * nq + i, 0)),
        out_shape=jax.ShapeDtypeStruct((batch * seq, width), F32),
        scratch_shapes=[
            pltpu.VMEM((MLA_HEADS * tq, 1), F32),
            pltpu.VMEM((MLA_HEADS * tq, 1), F32),
            pltpu.VMEM((MLA_HEADS * tq, r), F32),
        ],
        compiler_params=_params("parallel", "parallel", "arbitrary"),
        name="prompt_attention",
    )(qa, qr, kc, kr, wuv, z)


def _sample_attn_kernel(pt_ref, qa_ref, qr_ref, kcn_ref, krn_ref, *rest, t, pages):
    lat_refs = rest[:pages]
    rope_refs = rest[pages:2 * pages]
    wuv_ref, zg_ref, out_ref, m_ref, l_ref, acc_ref = rest[2 * pages:]
    j = pl.program_id(1)
    rows = MLA_HEADS * t
    qa = qa_ref[...].reshape(rows, qa_ref.shape[-1]).astype(BF16)
    qr = qr_ref[...].reshape(rows, LANES)[:, :QK_ROPE].astype(BF16)

    @pl.when(j == 0)
    def _():
        m_ref[...] = jnp.full(m_ref.shape, -jnp.inf, F32)
        l_ref[...] = jnp.zeros(l_ref.shape, F32)
        acc_ref[...] = jnp.zeros(acc_ref.shape, F32)
        kc = jnp.concatenate([kcn_ref[...], jnp.zeros((LANES - t, kcn_ref.shape[-1]), F32)], axis=0).astype(BF16)
        kr = jnp.concatenate([krn_ref[...], jnp.zeros((LANES - t, QK_ROPE), F32)], axis=0).astype(BF16)
        s = (_dot_nt(qa, kc) + _dot_nt(qr, kr)) * ATTN_SCALE
        q_tok = lax.broadcasted_iota(jnp.int32, (rows, 1), 0) % t
        k_tok = lax.broadcasted_iota(jnp.int32, (1, LANES), 1)
        s = jnp.where(k_tok <= q_tok, s, -jnp.inf)
        _online_softmax_update(s, kc, m_ref, l_ref, acc_ref)

    kcs = [ref[...].astype(BF16) for ref in lat_refs]
    s = jnp.concatenate(
        [_dot_nt(qa, kc) + _dot_nt(qr, ref[...].astype(BF16)) for kc, ref in zip(kcs, rope_refs)],
        axis=-1) * ATTN_SCALE
    _online_softmax_update(s, jnp.concatenate(kcs, axis=0), m_ref, l_ref, acc_ref)

    @pl.when(j == pl.num_programs(1) - 1)
    def _():
        o = acc_ref[...] / l_ref[...]
        ob = o.astype(BF16)
        heads = [_dot(ob, wuv_ref[h])[h * t:(h + 1) * t] for h in range(MLA_HEADS)]
        out_ref[...] = jnp.concatenate(heads, axis=-1) * _silu(zg_ref[...])


def _sample_attention(qa, qr, ckv, krope, cache_lat, cache_rope, layer, page_table, wuv, z, batch, t):
    assert t == SUBLANES
    r = qa.shape[-1]
    n_pages = page_table.shape[1]
    pages = min(PAGES_PER_STEP, n_pages)
    steps = n_pages // pages
    width = wuv.shape[0] * wuv.shape[2]
    pt = page_table.reshape(-1)

    def page_spec(p, last):
        return pl.BlockSpec((None, None, PAGE_SIZE, last),
                            lambda b, j, pt_ref: (layer, pt_ref[b * n_pages + j * pages + p], 0, 0))

    grid_spec = pltpu.PrefetchScalarGridSpec(
        num_scalar_prefetch=1,
        grid=(batch, steps),
        in_specs=[
            pl.BlockSpec((MLA_HEADS, t, r), lambda b, j, pt_ref: (0, b, 0)),
            pl.BlockSpec((MLA_HEADS, t, LANES), lambda b, j, pt_ref: (0, b, 0)),
            pl.BlockSpec((t, r), lambda b, j, pt_ref: (b, 0)),
            pl.BlockSpec((t, QK_ROPE), lambda b, j, pt_ref: (b, 0)),
            *[page_spec(p, r) for p in range(pages)],
            *[page_spec(p, QK_ROPE) for p in range(pages)],
            pl.BlockSpec(wuv.shape, lambda b, j, pt_ref: (0, 0, 0)),
            pl.BlockSpec((t, width), lambda b, j, pt_ref: (b, 2)),
        ],
        out_specs=pl.BlockSpec((t, width), lambda b, j, pt_ref: (b, 0)),
        scratch_shapes=[
            pltpu.VMEM((MLA_HEADS * t, 1), F32),
            pltpu.VMEM((MLA_HEADS * t, 1), F32),
            pltpu.VMEM((MLA_HEADS * t, r), F32),
        ],
    )
    return pl.pallas_call(
        functools.partial(_sample_attn_kernel, t=t, pages=pages),
        grid_spec=grid_spec,
        out_shape=jax.ShapeDtypeStruct((batch * t, width), F32),
        compiler_params=_params("parallel", "arbitrary"),
        name="sample_attention",
    )(pt, qa, qr, ckv, krope, *([cache_lat] * pages), *([cache_rope] * pages), wuv, z)


def _out_proj_kernel(a_ref, b_ref, c_ref, x_ref, wa_ref, wb_ref, wc_ref, g_ref, y_ref, *, final_norm):
    y = x_ref[...] + _dot(a_ref[...].astype(BF16), wa_ref[...])
    y = y + _dot(b_ref[...].astype(BF16), wb_ref[...])
    y = y + _dot(c_ref[...].astype(BF16), wc_ref[...])
    if final_norm:
        y = _rms(y, g_ref[...])
    y_ref[...] = y


def _out_proj(out_a, out_b, out_c, x, wa, wb, wc, g, final_norm):
    n, d = x.shape
    tm = min(ROW_TILE // 2, n)
    rows = lambda width: pl.BlockSpec((tm, width), lambda i: (i, 0))
    full = lambda arr: pl.BlockSpec(arr.shape, lambda i: (0, 0))
    return pl.pallas_call(
        functools.partial(_out_proj_kernel, final_norm=final_norm),
        grid=(n // tm,),
        in_specs=[rows(out_a.shape[1]), rows(out_b.shape[1]), rows(out_c.shape[1]), rows(d),
                  full(wa), full(wb), full(wc), full(g)],
        out_specs=rows(d),
        out_shape=jax.ShapeDtypeStruct((n, d), F32),
        compiler_params=_params("parallel"),
        name="out_proj",
    )(out_a, out_b, out_c, x, wa, wb, wc, g)


def _rope_table(pos, rows):
    half = QK_ROPE // 2
    freqs = ROPE_THETA ** (-jnp.arange(half, dtype=F32) / half)
    ang = pos.astype(F32)[:, None] * freqs[None, :]
    cos, sin = jnp.cos(ang), jnp.sin(ang)
    table = jnp.concatenate([cos, cos, -sin, sin], axis=-1)
    reps = max(1, rows // table.shape[0])
    return jnp.tile(table, (reps, 1))


def _block_diag(w):
    h, d, _ = w.shape
    eye = jnp.eye(h, dtype=w.dtype)
    return (eye[:, None, :, None] * w[:, :, None, :]).reshape(h * d, h * d)


def _swap_halves(w):
    half = w.shape[-1] // 2
    return jnp.concatenate([w[..., half:], w[..., :half]], axis=-1)


def _layer_weights(l, norm_g, w_in, w_out, lru_conv_w, lru_conv_b, lru_w_a, lru_b_a, lru_w_x, lru_b_x, lru_lambda,
                   mla_q_norm, mla_kv_norm, mla_w_uq, mla_w_uk, mla_w_uv, conv_dw_w, conv_dw_b, conv_ln_g,
                   conv_ln_b):
    lw = lru_conv_w.shape[2]
    q_rank = mla_q_norm.shape[1]
    kv_rank = mla_kv_norm.shape[1]
    mla_w = mla_w_uv.shape[2] * mla_w_uv.shape[3]
    cw = conv_dw_w.shape[2]
    sizes = (lw, lw, q_rank, kv_rank, QK_ROPE, mla_w, cw, cw, cw)
    offs = [0]
    for s in sizes:
        offs.append(offs[-1] + s)
    w = w_in[l]
    col = lambda k: w[:, offs[k]:offs[k + 1]]
    w_main = jnp.concatenate([col(0), col(1), col(2), col(3), col(5), col(6), col(7), col(8)], axis=1).astype(BF16)
    w_kr = jnp.concatenate([col(4), _swap_halves(col(4))], axis=1).astype(BF16)
    wuq = mla_w_uq[l]
    wn = wuq[:, :, :QK_NOPE].reshape(q_rank, -1).astype(BF16)
    wrope = wuq[:, :, QK_NOPE:]
    wr = jnp.concatenate([wrope, _swap_halves(wrope)], axis=-1).reshape(q_rank, -1).astype(BF16)
    wuk = jnp.transpose(mla_w_uk[l], (1, 2, 0)).astype(BF16)
    wuv = jnp.transpose(mla_w_uv[l], (1, 0, 2)).astype(BF16)
    wo = w_out[l].astype(BF16)
    row = lambda v: v[l][None, :]
    return dict(
        norm_g=row(norm_g), w_main=w_main, w_kr=w_kr,
        lru_cw=lru_conv_w[l], lru_cb=row(lru_conv_b),
        lru_wa=_block_diag(lru_w_a[l]).astype(BF16), lru_ba=row(lru_b_a),
        lru_wx=_block_diag(lru_w_x[l]).astype(BF16), lru_bx=row(lru_b_x), lru_lam=row(lru_lambda),
        qn=row(mla_q_norm), kvn=row(mla_kv_norm), wn=wn, wr=wr, wuk=wuk, wuv=wuv,
        dw_w=conv_dw_w[l], dw_b=row(conv_dw_b), ln_g=row(conv_ln_g), ln_b=row(conv_ln_b),
        wo_a=wo[:lw], wo_b=wo[lw:lw + mla_w], wo_c=wo[lw + mla_w:],
        conv_col0=(2 * lw + q_rank + kv_rank + mla_w) // cw,
    )


def _mixer_layer(x, p, layer, lru_buf, lru_h0, conv_buf, table, attend, batch, seq, q_dtype, final_g):
    z, zkr = _in_proj(x, p['norm_g'], p['w_main'], p['w_kr'])
    out_a, h_last, lru_buf_new = _lru(z, layer, lru_buf, lru_h0, p['lru_cw'], p['lru_cb'], p['lru_wa'], p['lru_ba'],
                                      p['lru_wx'], p['lru_bx'], p['lru_lam'], batch, seq)
    out_c, conv_buf_new = _conv(z, layer, conv_buf, p['dw_w'], p['dw_b'], p['ln_g'], p['ln_b'], batch, seq,
                                p['conv_col0'])
    qa, qr, ckv, ckv_bf, krope, krp = _mla_proj(z, zkr, p['qn'], p['kvn'], p['wn'], p['wr'], p['wuk'], table, q_dtype)
    out_b = attend(qa, qr, ckv, ckv_bf, krope, krp, p['wuv'], z)
    y = _out_proj(out_a, out_b, out_c, x, p['wo_a'], p['wo_b'], p['wo_c'],
                  final_g if final_g is not None else p['norm_g'], final_g is not None)
    return y, (ckv, krope, h_last[:, 0], lru_buf_new, conv_buf_new)


def kernel(x_prompt, x_sample, cache_kv_latent, cache_k_rope, page_table, state_lru_h, state_lru_conv, state_conv, norm_g, w_in, w_out, lru_conv_w, lru_conv_b, lru_w_a, lru_b_a, lru_w_x, lru_b_x, lru_lambda, mla_q_norm, mla_kv_norm, mla_w_uq, mla_w_uk, mla_w_uv, conv_dw_w, conv_dw_b, conv_ln_g, conv_ln_b, final_norm_g):
    b_p, seq, d_model = x_prompt.shape
    b_s, dec_seq, _ = x_sample.shape
    depth = norm_g.shape[0]
    lw = lru_conv_w.shape[2]
    cw = conv_dw_w.shape[2]
    past_len = page_table.shape[1] * PAGE_SIZE
    n_p, n_s = b_p * seq, b_s * dec_seq
    table_p = _rope_table(jnp.arange(seq, dtype=F32), min(ROW_TILE, n_p))
    table_s = _rope_table(past_len + jnp.arange(dec_seq, dtype=F32), min(ROW_TILE, n_s))
    zero_h = jnp.zeros((1, b_p, 1, lw), F32)
    zero_lbuf = jnp.zeros((1, b_p, LRU_CONV - 1, lw), F32)
    zero_cbuf = jnp.zeros((1, b_p, CONV_K - 1, cw), F32)
    h0_s = state_lru_h[:, :, None, :]
    final_g = final_norm_g[None, :]

    xp = x_prompt.reshape(n_p, d_model)
    xs = x_sample.reshape(n_s, d_model)
    st_p, st_s = [], []
    for l in range(depth):
        p = _layer_weights(l, norm_g, w_in, w_out, lru_conv_w, lru_conv_b, lru_w_a, lru_b_a, lru_w_x, lru_b_x,
                           lru_lambda, mla_q_norm, mla_kv_norm, mla_w_uq, mla_w_uk, mla_w_uv, conv_dw_w,
                           conv_dw_b, conv_ln_g, conv_ln_b)
        last = final_g if l == depth - 1 else None

        def attend_p(qa, qr, ckv, ckv_bf, krope, krp, wuv, z):
            return _prompt_attention(qa, qr, ckv_bf, krp, wuv, z, b_p, seq)

        def attend_s(qa, qr, ckv, ckv_bf, krope, krp, wuv, z, l=l):
            return _sample_attention(qa, qr, ckv, krope, cache_kv_latent, cache_k_rope, l, page_table, wuv, z,
                                     b_s, dec_seq)

        xp, sp = _mixer_layer(xp, p, 0, zero_lbuf, zero_h, zero_cbuf, table_p, attend_p, b_p, seq, BF16, last)
        xs, ss = _mixer_layer(xs, p, l, state_lru_conv, h0_s, state_conv, table_s, attend_s, b_s, dec_seq, F32, last)
        st_p.append(sp)
        st_s.append(ss)

    def stack(states, k, shape):
        return jnp.stack([s[k].reshape(shape) for s in states])

    r = mla_kv_norm.shape[1]
    return (xp.reshape(b_p, seq, d_model), xs.reshape(b_s, dec_seq, d_model),
            stack(st_p, 0, (b_p, seq, r)), stack(st_p, 1, (b_p, seq, QK_ROPE)), stack(st_p, 2, (b_p, lw)),
            stack(st_p, 3, (b_p, LRU_CONV - 1, lw)), stack(st_p, 4, (b_p, CONV_K - 1, cw)),
            stack(st_s, 0, (b_s, dec_seq, r)), stack(st_s, 1, (b_s, dec_seq, QK_ROPE)), stack(st_s, 2, (b_s, lw)),
            stack(st_s, 3, (b_s, LRU_CONV - 1, lw)), stack(st_s, 4, (b_s, CONV_K - 1, cw)))
```

```python
import functools

import jax
import jax.numpy as jnp
from jax import lax
from jax.experimental import pallas as pl
from jax.experimental.pallas import tpu as pltpu

F32 = jnp.float32
BF16 = jnp.bfloat16

LRU_HEADS = 8
LRU_CONV = 4
LRU_C = 8.0
MLA_HEADS = 8
QK_NOPE = 128
QK_ROPE = 64
ROPE_THETA = 10000.0
ATTN_SCALE = (QK_NOPE + QK_ROPE) ** -0.5
PAGE_SIZE = 128
CONV_K = 31
RMS_EPS = 1e-6
LN_EPS = 1e-5

LANES = 128
SUBLANES = 8
VMEM_LIMIT_BYTES = 56 * 1024 * 1024

ROW_TILE = 512
IN_ROW_TILE = 1024
IN_COL_TILE = 1536
SEQ_TILE = 256
LRU_SAMPLE_SEQS = 128
CONV_SAMPLE_SEQS = 32
Q_TILE = 128
KV_TILE = 512
PAGES_PER_CHUNK = 8
PAGES_PER_GROUP = 2


def _params(*sem):
    return pltpu.CompilerParams(dimension_semantics=sem, vmem_limit_bytes=VMEM_LIMIT_BYTES)


def _dot(a, b):
    return jnp.dot(a, b, preferred_element_type=F32)


def _dot_nt(a, b):
    return lax.dot_general(a, b, (((1,), (1,)), ((), ())), preferred_element_type=F32)


def _rms(x, g):
    return x * lax.rsqrt(jnp.mean(x * x, axis=-1, keepdims=True) + RMS_EPS) * g


def _silu(x):
    return x * jax.nn.sigmoid(x)


def _seq_block(batch, preferred):
    if batch <= preferred:
        return batch
    assert batch % preferred == 0
    return preferred


def _in_proj_kernel(x_ref, g_ref, w_ref, wkr_ref, z_ref, zkr_ref, hn_ref):
    @pl.when(pl.program_id(1) == 0)
    def _():
        hn = _rms(x_ref[...], g_ref[...]).astype(BF16)
        hn_ref[...] = hn
        zkr_ref[...] = _dot(hn, wkr_ref[...])

    z_ref[...] = _dot(hn_ref[...], w_ref[...])


def _in_proj(x, g, w_main, w_kr):
    n, d = x.shape
    cols = w_main.shape[1]
    tm = min(IN_ROW_TILE, n)
    tn = IN_COL_TILE
    return pl.pallas_call(
        _in_proj_kernel,
        grid=(n // tm, cols // tn),
        in_specs=[
            pl.BlockSpec((tm, d), lambda i, j: (i, 0)),
            pl.BlockSpec((1, d), lambda i, j: (0, 0)),
            pl.BlockSpec((d, tn), lambda i, j: (0, j)),
            pl.BlockSpec((d, LANES), lambda i, j: (0, 0)),
        ],
        out_specs=[
            pl.BlockSpec((tm, tn), lambda i, j: (i, j)),
            pl.BlockSpec((tm, LANES), lambda i, j: (i, 0)),
        ],
        out_shape=[jax.ShapeDtypeStruct((n, cols), F32), jax.ShapeDtypeStruct((n, LANES), F32)],
        scratch_shapes=[pltpu.VMEM((tm, d), BF16)],
        compiler_params=_params("parallel", "arbitrary"),
        name="in_proj",
    )(x, g, w_main, w_kr)


def _lru_gates(xc, wa_ref, ba_ref, wx_ref, bx_ref, lam_ref):
    xcb = xc.astype(BF16)
    r = jax.nn.sigmoid(_dot(xcb, wa_ref[...]) + ba_ref[...])
    i = jax.nn.sigmoid(_dot(xcb, wx_ref[...]) + bx_ref[...])
    softplus_neg_lam = jnp.log(1.0 + jnp.exp(-lam_ref[...]))
    a = jnp.exp(-LRU_C * r * softplus_neg_lam)
    return a, jnp.sqrt(1.0 - a * a) * i * xc


def _lru_kernel(zx_ref, zg_ref, buf_ref, h0_ref, cw_ref, cb_ref, wa_ref, ba_ref, wx_ref, bx_ref, lam_ref,
                out_ref, hlast_ref, nbuf_ref, xp_ref, h_ref, a_ref, u_ref, hs_ref, *, tl):
    pad = SUBLANES - (LRU_CONV - 1)

    @pl.when(pl.program_id(1) == 0)
    def _():
        xp_ref[pad:SUBLANES, :] = buf_ref[...]
        h_ref[...] = h0_ref[...]

    x = zx_ref[...]
    xp_ref[SUBLANES:SUBLANES + tl, :] = x
    xc = cb_ref[...] + cw_ref[LRU_CONV - 1:LRU_CONV, :] * x
    for k in range(LRU_CONV - 1):
        xc = xc + cw_ref[k:k + 1, :] * xp_ref[pad + k:pad + k + tl, :]
    tail = xp_ref[pad + tl:SUBLANES + tl, :]
    xp_ref[pad:SUBLANES, :] = tail
    nbuf_ref[...] = tail

    a, u = _lru_gates(xc, wa_ref, ba_ref, wx_ref, bx_ref, lam_ref)
    a_ref[...] = a
    u_ref[...] = u

    def step(t, h):
        h = a_ref[pl.ds(t, 1), :] * h + u_ref[pl.ds(t, 1), :]
        hs_ref[pl.ds(t, 1), :] = h
        return h

    h = lax.fori_loop(0, tl, step, h_ref[...], unroll=SUBLANES)
    h_ref[...] = h
    hlast_ref[...] = h
    out_ref[...] = hs_ref[...] * _silu(zg_ref[...])


def _lru_prompt(z, buf, h0, p, batch, seq):
    w = p['lru_cw'].shape[1]
    tl = min(SEQ_TILE, seq)
    nc = seq // tl
    row = lambda b, c: (b * nc + c, 0)
    vec = pl.BlockSpec((1, w), lambda b, c: (0, 0))
    mat = pl.BlockSpec((w, w), lambda b, c: (0, 0))
    out_a, h_last, nbuf = pl.pallas_call(
        functools.partial(_lru_kernel, tl=tl),
        grid=(batch, nc),
        in_specs=[
            pl.BlockSpec((tl, w), row),
            pl.BlockSpec((tl, w), lambda b, c: (b * nc + c, 1)),
            pl.BlockSpec((None, LRU_CONV - 1, w), lambda b, c: (b, 0, 0)),
            pl.BlockSpec((None, 1, w), lambda b, c: (b, 0, 0)),
            pl.BlockSpec((LRU_CONV, w), lambda b, c: (0, 0)),
            vec, mat, vec, mat, vec, vec,
        ],
        out_specs=[
            pl.BlockSpec((tl, w), row),
            pl.BlockSpec((None, 1, w), lambda b, c: (b, 0, 0)),
            pl.BlockSpec((None, LRU_CONV - 1, w), lambda b, c: (b, 0, 0)),
        ],
        out_shape=[
            jax.ShapeDtypeStruct((batch * seq, w), F32),
            jax.ShapeDtypeStruct((batch, 1, w), F32),
            jax.ShapeDtypeStruct((batch, LRU_CONV - 1, w), F32),
        ],
        scratch_shapes=[
            pltpu.VMEM((tl + SUBLANES, w), F32),
            pltpu.VMEM((1, w), F32),
            pltpu.VMEM((tl, w), F32),
            pltpu.VMEM((tl, w), F32),
            pltpu.VMEM((tl, w), F32),
        ],
        compiler_params=_params("parallel", "arbitrary"),
        name="lru_prompt",
    )(z, z, buf, h0, p['lru_cw'], p['lru_cb'], p['lru_wa'], p['lru_ba'], p['lru_wx'], p['lru_bx'], p['lru_lam'])
    return out_a, h_last[:, 0], nbuf


def _token_slab(refs, k, bb, t):
    return jnp.concatenate([ref[pl.ds(k, bb, stride=t), :] for ref in refs], axis=-1)


def _store_token_slab(stage_ref, k, bb, t, val):
    for c in range(stage_ref.shape[0]):
        stage_ref[c, pl.ds(k, bb, stride=t), :] = val[:, c * LANES:(c + 1) * LANES]


def _unstage(stage_ref, out_ref):
    for c in range(stage_ref.shape[0]):
        out_ref[:, c * LANES:(c + 1) * LANES] = stage_ref[c]


def _lane_tile_specs(rows, width, col0):
    nl = width // LANES
    return [pl.BlockSpec((rows, LANES), lambda i, c=c: (i, col0 * nl + c)) for c in range(nl)]


def _lru_sample_kernel(*refs, t, bb, nl):
    zx_refs, zg_refs = refs[:nl], refs[nl:2 * nl]
    (buf_ref, h0_ref, cw_ref, cb_ref, wa_ref, ba_ref, wx_ref, bx_ref, lam_ref,
     out_ref, hlast_ref, nbuf_ref, stage_ref) = refs[2 * nl:]
    xp = [buf_ref[k] for k in range(LRU_CONV - 1)] + [_token_slab(zx_refs, k, bb, t) for k in range(t)]
    h = h0_ref[...]
    for k in range(t):
        xc = cb_ref[...] + cw_ref[0:1, :] * xp[k]
        for j in range(1, LRU_CONV):
            xc = xc + cw_ref[j:j + 1, :] * xp[k + j]
        a, u = _lru_gates(xc, wa_ref, ba_ref, wx_ref, bx_ref, lam_ref)
        h = a * h + u
        _store_token_slab(stage_ref, k, bb, t, h * _silu(_token_slab(zg_refs, k, bb, t)))
    _unstage(stage_ref, out_ref)
    hlast_ref[...] = h
    for j in range(LRU_CONV - 1):
        nbuf_ref[j] = xp[t + j]


def _lru_sample(z, layer, buf_t, h0, p, batch, t):
    w = p['lru_cw'].shape[1]
    nl = w // LANES
    bb = _seq_block(batch, LRU_SAMPLE_SEQS)
    vec = pl.BlockSpec((1, w), lambda i: (0, 0))
    mat = pl.BlockSpec((w, w), lambda i: (0, 0))
    out_a, h_last, nbuf = pl.pallas_call(
        functools.partial(_lru_sample_kernel, t=t, bb=bb, nl=nl),
        grid=(batch // bb,),
        in_specs=[
            *_lane_tile_specs(bb * t, w, 0),
            *_lane_tile_specs(bb * t, w, 1),
            pl.BlockSpec((None, LRU_CONV - 1, bb, w), lambda i: (layer, 0, i, 0)),
            pl.BlockSpec((None, bb, w), lambda i: (layer, i, 0)),
            pl.BlockSpec((LRU_CONV, w), lambda i: (0, 0)),
            vec, mat, vec, mat, vec, vec,
        ],
        out_specs=[
            pl.BlockSpec((bb * t, w), lambda i: (i, 0)),
            pl.BlockSpec((bb, w), lambda i: (i, 0)),
            pl.BlockSpec((LRU_CONV - 1, bb, w), lambda i: (0, i, 0)),
        ],
        out_shape=[
            jax.ShapeDtypeStruct((batch * t, w), F32),
            jax.ShapeDtypeStruct((batch, w), F32),
            jax.ShapeDtypeStruct((LRU_CONV - 1, batch, w), F32),
        ],
        scratch_shapes=[pltpu.VMEM((nl, bb * t, LANES), F32)],
        compiler_params=_params("parallel"),
        name="lru_sample",
    )(*([z] * (2 * nl)), buf_t, h0, p['lru_cw'], p['lru_cb'], p['lru_wa'], p['lru_ba'], p['lru_wx'], p['lru_bx'],
      p['lru_lam'])
    return out_a, h_last, jnp.transpose(nbuf, (1, 0, 2))


CONV_HIST = CONV_K - 1
CONV_PAD = 32


def _ln_silu_gate(y, lng_ref, lnb_ref, gate):
    mu = jnp.mean(y, axis=-1, keepdims=True)
    yc = y - mu
    yn = yc * lax.rsqrt(jnp.mean(yc * yc, axis=-1, keepdims=True) + LN_EPS) * lng_ref[...] + lnb_ref[...]
    return _silu(yn) * _silu(gate)


def _conv_kernel(zv_ref, zglu_ref, zg_ref, buf_ref, w_ref, b_ref, lng_ref, lnb_ref,
                 out_ref, nbuf_ref, s_ref, *, tl):
    base = CONV_PAD - CONV_HIST

    @pl.when(pl.program_id(1) == 0)
    def _():
        s_ref[base:CONV_PAD, :] = buf_ref[...]

    s_ref[CONV_PAD:CONV_PAD + tl, :] = zv_ref[...] * jax.nn.sigmoid(zglu_ref[...])
    y = b_ref[...] + w_ref[0:1, :] * s_ref[base:base + tl, :]
    for k in range(1, CONV_K):
        y = y + w_ref[k:k + 1, :] * s_ref[base + k:base + k + tl, :]
    tail = s_ref[base + tl:CONV_PAD + tl, :]
    s_ref[base:CONV_PAD, :] = tail
    nbuf_ref[...] = tail
    out_ref[...] = _ln_silu_gate(y, lng_ref, lnb_ref, zg_ref[...])


def _conv_prompt(z, buf, p, batch, seq):
    cw = p['dw_w'].shape[1]
    col0 = p['conv_col0']
    tl = min(SEQ_TILE, seq)
    nc = seq // tl
    vec = pl.BlockSpec((1, cw), lambda b_, c: (0, 0))
    zspec = lambda off: pl.BlockSpec((tl, cw), lambda b_, c: (b_ * nc + c, col0 + off))
    return pl.pallas_call(
        functools.partial(_conv_kernel, tl=tl),
        grid=(batch, nc),
        in_specs=[
            zspec(0), zspec(1), zspec(2),
            pl.BlockSpec((None, CONV_HIST, cw), lambda b_, c: (b_, 0, 0)),
            pl.BlockSpec((CONV_K, cw), lambda b_, c: (0, 0)),
            vec, vec, vec,
        ],
        out_specs=[
            pl.BlockSpec((tl, cw), lambda b_, c: (b_ * nc + c, 0)),
            pl.BlockSpec((None, CONV_HIST, cw), lambda b_, c: (b_, 0, 0)),
        ],
        out_shape=[
            jax.ShapeDtypeStruct((batch * seq, cw), F32),
            jax.ShapeDtypeStruct((batch, CONV_HIST, cw), F32),
        ],
        scratch_shapes=[pltpu.VMEM((tl + CONV_PAD, cw), F32)],
        compiler_params=_params("parallel", "arbitrary"),
        name="conv_prompt",
    )(z, z, z, buf, p['dw_w'], p['dw_b'], p['ln_g'], p['ln_b'])


def _conv_sample_kernel(*refs, t, bb, nl):
    zv_refs, zglu_refs, zg_refs = refs[:nl], refs[nl:2 * nl], refs[2 * nl:3 * nl]
    buf_ref, w_ref, b_ref, lng_ref, lnb_ref, out_ref, nbuf_ref, u_ref, stage_ref = refs[3 * nl:]
    for k in range(t):
        u_ref[k] = _token_slab(zv_refs, k, bb, t) * jax.nn.sigmoid(_token_slab(zglu_refs, k, bb, t))

    def tap(j):
        return buf_ref[j] if j < CONV_HIST else u_ref[j - CONV_HIST]

    for k in range(t):
        y = b_ref[...] + w_ref[0:1, :] * tap(k)
        for j in range(1, CONV_K):
            y = y + w_ref[j:j + 1, :] * tap(k + j)
        _store_token_slab(stage_ref, k, bb, t, _ln_silu_gate(y, lng_ref, lnb_ref, _token_slab(zg_refs, k, bb, t)))
    _unstage(stage_ref, out_ref)
    for j in range(CONV_HIST):
        nbuf_ref[j] = tap(t + j)


def _conv_sample(z, layer, buf_t, p, batch, t):
    cw = p['dw_w'].shape[1]
    nl = cw // LANES
    col0 = p['conv_col0']
    bb = _seq_block(batch, CONV_SAMPLE_SEQS)
    vec = pl.BlockSpec((1, cw), lambda i: (0, 0))
    out_c, nbuf = pl.pallas_call(
        functools.partial(_conv_sample_kernel, t=t, bb=bb, nl=nl),
        grid=(batch // bb,),
        in_specs=[
            *_lane_tile_specs(bb * t, cw, col0),
            *_lane_tile_specs(bb * t, cw, col0 + 1),
            *_lane_tile_specs(bb * t, cw, col0 + 2),
            pl.BlockSpec((None, CONV_HIST, bb, cw), lambda i: (layer, 0, i, 0)),
            pl.BlockSpec((CONV_K, cw), lambda i: (0, 0)),
            vec, vec, vec,
        ],
        out_specs=[
            pl.BlockSpec((bb * t, cw), lambda i: (i, 0)),
            pl.BlockSpec((CONV_HIST, bb, cw), lambda i: (0, i, 0)),
        ],
        out_shape=[
            jax.ShapeDtypeStruct((batch * t, cw), F32),
            jax.ShapeDtypeStruct((CONV_HIST, batch, cw), F32),
        ],
        scratch_shapes=[pltpu.VMEM((t, bb, cw), F32), pltpu.VMEM((nl, bb * t, LANES), F32)],
        compiler_params=_params("parallel"),
        name="conv_sample",
    )(*([z] * (3 * nl)), buf_t, p['dw_w'], p['dw_b'], p['ln_g'], p['ln_b'])
    return out_c, jnp.transpose(nbuf, (1, 0, 2))


def _rope128(v, table):
    a = v * table
    return a + pltpu.roll(a, QK_ROPE, 1)


def _mla_proj_kernel(zq_ref, zkv_ref, zkr_ref, qn_ref, kvn_ref, wn_ref, wr_ref, wuk_ref, tab_ref,
                     qa_ref, qr_ref, ckv_ref, ckvb_ref, krope_ref, krp_ref):
    tab = tab_ref[...]
    cq = _rms(zq_ref[...], qn_ref[...]).astype(BF16)
    q_nope = _dot(cq, wn_ref[...])
    q_rope = _dot(cq, wr_ref[...])
    for h in range(MLA_HEADS):
        qh = q_nope[:, h * QK_NOPE:(h + 1) * QK_NOPE].astype(BF16)
        qa_ref[h] = _dot(qh, wuk_ref[h]).astype(qa_ref.dtype)
        qr_ref[h] = _rope128(q_rope[:, h * LANES:(h + 1) * LANES], tab).astype(qr_ref.dtype)
    ckv = _rms(zkv_ref[...], kvn_ref[...])
    ckv_ref[...] = ckv
    ckvb_ref[...] = ckv.astype(BF16)
    kro = _rope128(zkr_ref[...], tab)
    krope_ref[...] = kro[:, :QK_ROPE]
    lane = lax.broadcasted_iota(jnp.int32, kro.shape, 1)
    krp_ref[...] = jnp.where(lane < QK_ROPE, kro, 0.0).astype(BF16)


def _mla_proj(z, zkr, p, table, q_dtype):
    n = z.shape[0]
    r = p['qn'].shape[1]
    tm = min(ROW_TILE, n)
    nt = table.shape[0] // tm
    wn, wr, wuk = p['wn'], p['wr'], p['wuk']
    full = lambda shape: pl.BlockSpec(shape, lambda i: (0,) * len(shape))
    return pl.pallas_call(
        _mla_proj_kernel,
        grid=(n // tm,),
        in_specs=[
            pl.BlockSpec((tm, r), lambda i: (i, 2)),
            pl.BlockSpec((tm, r), lambda i: (i, 3)),
            pl.BlockSpec((tm, LANES), lambda i: (i, 0)),
            full((1, r)), full((1, r)),
            full(wn.shape), full(wr.shape), full(wuk.shape),
            pl.BlockSpec((tm, LANES), lambda i: (i % nt, 0)),
        ],
        out_specs=[
            pl.BlockSpec((MLA_HEADS, tm, r), lambda i: (0, i, 0)),
            pl.BlockSpec((MLA_HEADS, tm, LANES), lambda i: (0, i, 0)),
            pl.BlockSpec((tm, r), lambda i: (i, 0)),
            pl.BlockSpec((tm, r), lambda i: (i, 0)),
            pl.BlockSpec((tm, QK_ROPE), lambda i: (i, 0)),
            pl.BlockSpec((tm, LANES), lambda i: (i, 0)),
        ],
        out_shape=[
            jax.ShapeDtypeStruct((MLA_HEADS, n, r), q_dtype),
            jax.ShapeDtypeStruct((MLA_HEADS, n, LANES), q_dtype),
            jax.ShapeDtypeStruct((n, r), F32),
            jax.ShapeDtypeStruct((n, r), BF16),
            jax.ShapeDtypeStruct((n, QK_ROPE), F32),
            jax.ShapeDtypeStruct((n, LANES), BF16),
        ],
        compiler_params=_params("parallel"),
        name="mla_proj",
    )(z, z, zkr, p['qn'], p['kvn'], wn, wr, wuk, table)


def _local_softmax(s, v):
    m = jnp.max(s, axis=-1, keepdims=True)
    p = jnp.exp(s - m)
    return m, jnp.sum(p, axis=-1, keepdims=True), _dot(p.astype(BF16), v)


def _merge_softmax(parts):
    m = parts[0][0]
    for part in parts[1:]:
        m = jnp.maximum(m, part[0])
    l = acc = None
    for m_g, l_g, acc_g in parts:
        w = jnp.exp(m_g - m)
        l = l_g * w if l is None else l + l_g * w
        acc = acc_g * w if acc is None else acc + acc_g * w
    return m, l, acc


def _online_softmax_update(s, v, m_ref, l_ref, acc_ref):
    m_old = m_ref[...]
    m_new = jnp.maximum(m_old, jnp.max(s, axis=-1, keepdims=True))
    corr = jnp.exp(m_old - m_new)
    p = jnp.exp(s - m_new)
    l_ref[...] = l_ref[...] * corr + jnp.sum(p, axis=-1, keepdims=True)
    acc_ref[...] = acc_ref[...] * corr + _dot(p.astype(BF16), v)
    m_ref[...] = m_new


def _prompt_attn_kernel(qa_ref, qr_ref, kc_ref, kr_ref, wuv_ref, zg_ref, out_ref, m_ref, l_ref, acc_ref,
                        *, tq, tk):
    i = pl.program_id(1)
    j = pl.program_id(2)
    last = (i * tq + tq - 1) // tk
    rows = MLA_HEADS * tq

    @pl.when(j == 0)
    def _():
        m_ref[...] = jnp.full(m_ref.shape, -jnp.inf, F32)
        l_ref[...] = jnp.zeros(l_ref.shape, F32)
        acc_ref[...] = jnp.zeros(acc_ref.shape, F32)

    @pl.when(j <= last)
    def _():
        qa = qa_ref[...].reshape(rows, qa_ref.shape[-1])
        qr = qr_ref[...].reshape(rows, LANES)
        s = (_dot_nt(qa, kc_ref[...]) + _dot_nt(qr, kr_ref[...])) * ATTN_SCALE
        q_pos = i * tq + (lax.broadcasted_iota(jnp.int32, (rows, 1), 0) & (tq - 1))
        k_pos = j * tk + lax.broadcasted_iota(jnp.int32, (1, tk), 1)
        s = jnp.where(k_pos <= q_pos, s, -jnp.inf)
        _online_softmax_update(s, kc_ref[...], m_ref, l_ref, acc_ref)

    @pl.when(j == last)
    def _():
        ob = (acc_ref[...] / l_ref[...]).astype(BF16)
        heads = [_dot(ob[h * tq:(h + 1) * tq], wuv_ref[h]) for h in range(MLA_HEADS)]
        out_ref[...] = jnp.concatenate(heads, axis=-1) * _silu(zg_ref[...])


def _prompt_attention(qa, qr, kc, kr, wuv, z, batch, seq):
    r = qa.shape[-1]
    tq = min(Q_TILE, seq)
    tk = min(KV_TILE, seq)
    assert tq & (tq - 1) == 0
    nq, nk = seq // tq, seq // tk
    width = wuv.shape[0] * wuv.shape[2]
    kv_row = lambda b, i, j: (b * nk + jnp.minimum(j, (i * tq + tq - 1) // tk), 0)
    return pl.pallas_call(
        functools.partial(_prompt_attn_kernel, tq=tq, tk=tk),
        grid=(batch, nq, nk),
        in_specs=[
            pl.BlockSpec((MLA_HEADS, tq, r), lambda b, i, j: (0, b * nq + i, 0)),
            pl.BlockSpec((MLA_HEADS, tq, LANES), lambda b, i, j: (0, b * nq + i, 0)),
            pl.BlockSpec((tk, r), kv_row),
            pl.BlockSpec((tk, LANES), kv_row),
            pl.BlockSpec(wuv.shape, lambda b, i, j: (0, 0, 0)),
            pl.BlockSpec((tq, width), lambda b, i, j: (b * nq + i, 2)),
        ],
        out_specs=pl.BlockSpec((tq, width), lambda b, i, j: (b * nq + i, 0)),
        out_shape=jax.ShapeDtypeStruct((batch * seq, width), F32),
        scratch_shapes=[
            pltpu.VMEM((MLA_HEADS * tq, 1), F32),
            pltpu.VMEM((MLA_HEADS * tq, 1), F32),
            pltpu.VMEM((MLA_HEADS * tq, r), F32),
        ],
        compiler_params=_params("parallel", "parallel", "arbitrary"),
        name="prompt_attention",
    )(qa, qr, kc, kr, wuv, z)


def _sample_attn_kernel(pt_ref, qa_ref, qr_ref, kcn_ref, krn_ref, wuv_ref, zg_ref, lat_hbm, rope_hbm,
                        out_ref, lat_buf, rope_buf, sem, m_ref, l_ref, acc_ref,
                        *, layer, t, n_pages, chunk, group):
    b = pl.program_id(0)
    n_seq = pl.num_programs(0)
    n_chunks = n_pages // chunk
    rows = MLA_HEADS * t
    r = qa_ref.shape[-1]

    def page_copies(seq, c, slot):
        copies = []
        for g in range(chunk):
            page = pt_ref[seq * n_pages + c * chunk + g]
            copies.append(pltpu.make_async_copy(lat_hbm.at[layer, page], lat_buf.at[slot, g], sem.at[0, slot]))
            copies.append(pltpu.make_async_copy(rope_hbm.at[layer, page], rope_buf.at[slot, g], sem.at[1, slot]))
        return copies

    @pl.when(b == 0)
    def _():
        for cp in page_copies(0, 0, 0):
            cp.start()

    qa = qa_ref[...].reshape(rows, r).astype(BF16)
    qr = qr_ref[...].reshape(rows, LANES)[:, :QK_ROPE].astype(BF16)

    kc = jnp.concatenate([kcn_ref[...], jnp.zeros((LANES - t, r), F32)], axis=0).astype(BF16)
    kr = jnp.concatenate([krn_ref[...], jnp.zeros((LANES - t, QK_ROPE), F32)], axis=0).astype(BF16)
    s = (_dot_nt(qa, kc) + _dot_nt(qr, kr)) * ATTN_SCALE
    q_tok = lax.broadcasted_iota(jnp.int32, (rows, 1), 0) % t
    k_tok = lax.broadcasted_iota(jnp.int32, (1, LANES), 1)
    m0, l0, acc0 = _local_softmax(jnp.where(k_tok <= q_tok, s, -jnp.inf), kc)
    m_ref[...] = m0
    l_ref[...] = l0
    acc_ref[...] = acc0

    def chunk_step(c, carry):
        slot = c % 2
        last = c == n_chunks - 1
        nxt_seq = jnp.where(last, b + 1, b)
        nxt_c = jnp.where(last, 0, c + 1)

        @pl.when(nxt_seq < n_seq)
        def _():
            for cp in page_copies(nxt_seq, nxt_c, 1 - slot):
                cp.start()

        for cp in page_copies(b, c, slot):
            cp.wait()

        keys, scores = [], []
        for g0 in range(0, chunk, group):
            kb = lat_buf[slot, g0:g0 + group].reshape(group * PAGE_SIZE, r).astype(BF16)
            s_rope = jnp.concatenate(
                [_dot(qr, rope_buf[slot, g0 + g].astype(BF16)) for g in range(group)], axis=-1)
            keys.append(kb)
            scores.append((_dot_nt(qa, kb) + s_rope) * ATTN_SCALE)
        parts = [(m_ref[...], l_ref[...], acc_ref[...])]
        parts += [_local_softmax(s, kb) for s, kb in zip(scores, keys)]
        m, l, acc = _merge_softmax(parts)
        m_ref[...] = m
        l_ref[...] = l
        acc_ref[...] = acc
        return carry

    lax.fori_loop(0, n_chunks, chunk_step, 0)

    ob = (acc_ref[...] / l_ref[...]).astype(BF16)
    heads = [_dot(ob, wuv_ref[h])[h * t:(h + 1) * t] for h in range(MLA_HEADS)]
    out_ref[...] = jnp.concatenate(heads, axis=-1) * _silu(zg_ref[...])


def _sample_attention(qa, qr, ckv, krope, cache_lat, cache_rope_t, layer, page_table, wuv, z, batch, t):
    assert t == SUBLANES
    r = qa.shape[-1]
    n_pages = page_table.shape[1]
    chunk = min(PAGES_PER_CHUNK, n_pages)
    group = min(PAGES_PER_GROUP, chunk)
    assert n_pages % chunk == 0 and (n_pages // chunk) % 2 == 0 and chunk % group == 0
    width = wuv.shape[0] * wuv.shape[2]
    grid_spec = pltpu.PrefetchScalarGridSpec(
        num_scalar_prefetch=1,
        grid=(batch,),
        in_specs=[
            pl.BlockSpec((MLA_HEADS, t, r), lambda b, pt_ref: (0, b, 0)),
            pl.BlockSpec((MLA_HEADS, t, LANES), lambda b, pt_ref: (0, b, 0)),
            pl.BlockSpec((t, r), lambda b, pt_ref: (b, 0)),
            pl.BlockSpec((t, QK_ROPE), lambda b, pt_ref: (b, 0)),
            pl.BlockSpec(wuv.shape, lambda b, pt_ref: (0, 0, 0)),
            pl.BlockSpec((t, width), lambda b, pt_ref: (b, 2)),
            pl.BlockSpec(memory_space=pl.ANY),
            pl.BlockSpec(memory_space=pl.ANY),
        ],
        out_specs=pl.BlockSpec((t, width), lambda b, pt_ref: (b, 0)),
        scratch_shapes=[
            pltpu.VMEM((2, chunk, PAGE_SIZE, r), F32),
            pltpu.VMEM((2, chunk, QK_ROPE, PAGE_SIZE), F32),
            pltpu.SemaphoreType.DMA((2, 2)),
            pltpu.VMEM((MLA_HEADS * t, 1), F32),
            pltpu.VMEM((MLA_HEADS * t, 1), F32),
            pltpu.VMEM((MLA_HEADS * t, r), F32),
        ],
    )
    return pl.pallas_call(
        functools.partial(_sample_attn_kernel, layer=layer, t=t, n_pages=n_pages, chunk=chunk, group=group),
        grid_spec=grid_spec,
        out_shape=jax.ShapeDtypeStruct((batch * t, width), F32),
        compiler_params=_params("arbitrary"),
        name="sample_attention",
    )(page_table.reshape(-1), qa, qr, ckv, krope, wuv, z, cache_lat, cache_rope_t)


def _out_proj_kernel(a_ref, b_ref, c_ref, x_ref, wa_ref, wb_ref, wc_ref, g_ref, y_ref, *, final_norm):
    y = x_ref[...] + _dot(a_ref[...].astype(BF16), wa_ref[...])
    y = y + _dot(b_ref[...].astype(BF16), wb_ref[...])
    y = y + _dot(c_ref[...].astype(BF16), wc_ref[...])
    if final_norm:
        y = _rms(y, g_ref[...])
    y_ref[...] = y


def _out_proj(out_a, out_b, out_c, x, wa, wb, wc, g, final_norm):
    n, d = x.shape
    tm = min(ROW_TILE // 2, n)
    rows = lambda width: pl.BlockSpec((tm, width), lambda i: (i, 0))
    full = lambda arr: pl.BlockSpec(arr.shape, lambda i: (0, 0))
    return pl.pallas_call(
        functools.partial(_out_proj_kernel, final_norm=final_norm),
        grid=(n // tm,),
        in_specs=[rows(out_a.shape[1]), rows(out_b.shape[1]), rows(out_c.shape[1]), rows(d),
                  full(wa), full(wb), full(wc), full(g)],
        out_specs=rows(d),
        out_shape=jax.ShapeDtypeStruct((n, d), F32),
        compiler_params=_params("parallel"),
        name="out_proj",
    )(out_a, out_b, out_c, x, wa, wb, wc, g)


def _rope_table(pos, rows):
    half = QK_ROPE // 2
    freqs = ROPE_THETA ** (-jnp.arange(half, dtype=F32) / half)
    ang = pos.astype(F32)[:, None] * freqs[None, :]
    cos, sin = jnp.cos(ang), jnp.sin(ang)
    table = jnp.concatenate([cos, cos, -sin, sin], axis=-1)
    reps = max(1, rows // table.shape[0])
    return jnp.tile(table, (reps, 1))


def _block_diag(w):
    h, d, _ = w.shape
    eye = jnp.eye(h, dtype=w.dtype)
    return (eye[:, None, :, None] * w[:, :, None, :]).reshape(h * d, h * d)


def _swap_halves(w):
    half = w.shape[-1] // 2
    return jnp.concatenate([w[..., half:], w[..., :half]], axis=-1)


def _layer_weights(l, norm_g, w_in, w_out, lru_conv_w, lru_conv_b, lru_w_a, lru_b_a, lru_w_x, lru_b_x, lru_lambda,
                   mla_q_norm, mla_kv_norm, mla_w_uq, mla_w_uk, mla_w_uv, conv_dw_w, conv_dw_b, conv_ln_g,
                   conv_ln_b):
    lw = lru_conv_w.shape[2]
    q_rank = mla_q_norm.shape[1]
    kv_rank = mla_kv_norm.shape[1]
    mla_w = mla_w_uv.shape[2] * mla_w_uv.shape[3]
    cw = conv_dw_w.shape[2]
    sizes = (lw, lw, q_rank, kv_rank, QK_ROPE, mla_w, cw, cw, cw)
    offs = [0]
    for s in sizes:
        offs.append(offs[-1] + s)
    w = w_in[l]
    col = lambda k: w[:, offs[k]:offs[k + 1]]
    w_main = jnp.concatenate([col(0), col(1), col(2), col(3), col(5), col(6), col(7), col(8)], axis=1).astype(BF16)
    w_kr = jnp.concatenate([col(4), _swap_halves(col(4))], axis=1).astype(BF16)
    wuq = mla_w_uq[l]
    wn = wuq[:, :, :QK_NOPE].reshape(q_rank, -1).astype(BF16)
    wrope = wuq[:, :, QK_NOPE:]
    wr = jnp.concatenate([wrope, _swap_halves(wrope)], axis=-1).reshape(q_rank, -1).astype(BF16)
    wuk = jnp.transpose(mla_w_uk[l], (1, 2, 0)).astype(BF16)
    wuv = jnp.transpose(mla_w_uv[l], (1, 0, 2)).astype(BF16)
    wo = w_out[l].astype(BF16)
    row = lambda v: v[l][None, :]
    return dict(
        norm_g=row(norm_g), w_main=w_main, w_kr=w_kr,
        lru_cw=lru_conv_w[l], lru_cb=row(lru_conv_b),
        lru_wa=_block_diag(lru_w_a[l]).astype(BF16), lru_ba=row(lru_b_a),
        lru_wx=_block_diag(lru_w_x[l]).astype(BF16), lru_bx=row(lru_b_x), lru_lam=row(lru_lambda),
        qn=row(mla_q_norm), kvn=row(mla_kv_norm), wn=wn, wr=wr, wuk=wuk, wuv=wuv,
        dw_w=conv_dw_w[l], dw_b=row(conv_dw_b), ln_g=row(conv_ln_g), ln_b=row(conv_ln_b),
        wo_a=wo[:lw], wo_b=wo[lw:lw + mla_w], wo_c=wo[lw + mla_w:],
        conv_col0=(2 * lw + q_rank + kv_rank + mla_w) // cw,
    )


def _mixer_layer(x, p, table, lru_fn, conv_fn, attend, q_dtype, final_g):
    z, zkr = _in_proj(x, p['norm_g'], p['w_main'], p['w_kr'])
    out_a, h_last, lru_buf_new = lru_fn(z, p)
    out_c, conv_buf_new = conv_fn(z, p)
    qa, qr, ckv, ckv_bf, krope, krp = _mla_proj(z, zkr, p, table, q_dtype)
    out_b = attend(qa, qr, ckv, ckv_bf, krope, krp, p['wuv'], z)
    y = _out_proj(out_a, out_b, out_c, x, p['wo_a'], p['wo_b'], p['wo_c'],
                  final_g if final_g is not None else p['norm_g'], final_g is not None)
    return y, (ckv, krope, h_last, lru_buf_new, conv_buf_new)


def kernel(x_prompt, x_sample, cache_kv_latent, cache_k_rope, page_table, state_lru_h, state_lru_conv, state_conv, norm_g, w_in, w_out, lru_conv_w, lru_conv_b, lru_w_a, lru_b_a, lru_w_x, lru_b_x, lru_lambda, mla_q_norm, mla_kv_norm, mla_w_uq, mla_w_uk, mla_w_uv, conv_dw_w, conv_dw_b, conv_ln_g, conv_ln_b, final_norm_g):
    b_p, seq, d_model = x_prompt.shape
    b_s, dec_seq, _ = x_sample.shape
    depth = norm_g.shape[0]
    lw = lru_conv_w.shape[2]
    cw = conv_dw_w.shape[2]
    past_len = page_table.shape[1] * PAGE_SIZE
    n_p, n_s = b_p * seq, b_s * dec_seq
    table_p = _rope_table(jnp.arange(seq, dtype=F32), min(ROW_TILE, n_p))
    table_s = _rope_table(past_len + jnp.arange(dec_seq, dtype=F32), min(ROW_TILE, n_s))
    zero_h = jnp.zeros((b_p, 1, lw), F32)
    zero_lbuf = jnp.zeros((b_p, LRU_CONV - 1, lw), F32)
    zero_cbuf = jnp.zeros((b_p, CONV_K - 1, cw), F32)
    lru_conv_t = jnp.transpose(state_lru_conv, (0, 2, 1, 3))
    conv_t = jnp.transpose(state_conv, (0, 2, 1, 3))
    cache_rope_t = jnp.swapaxes(cache_k_rope, 2, 3)
    final_g = final_norm_g[None, :]

    xp = x_prompt.reshape(n_p, d_model)
    xs = x_sample.reshape(n_s, d_model)
    st_p, st_s = [], []
    for l in range(depth):
        p = _layer_weights(l, norm_g, w_in, w_out, lru_conv_w, lru_conv_b, lru_w_a, lru_b_a, lru_w_x, lru_b_x,
                           lru_lambda, mla_q_norm, mla_kv_norm, mla_w_uq, mla_w_uk, mla_w_uv, conv_dw_w,
                           conv_dw_b, conv_ln_g, conv_ln_b)
        last = final_g if l == depth - 1 else None

        def attend_p(qa, qr, ckv, ckv_bf, krope, krp, wuv, z):
            return _prompt_attention(qa, qr, ckv_bf, krp, wuv, z, b_p, seq)

        def attend_s(qa, qr, ckv, ckv_bf, krope, krp, wuv, z, l=l):
            return _sample_attention(qa, qr, ckv, krope, cache_kv_latent, cache_rope_t, l, page_table, wuv, z,
                                     b_s, dec_seq)

        xp, sp = _mixer_layer(
            xp, p, table_p,
            lambda z, p: _lru_prompt(z, zero_lbuf, zero_h, p, b_p, seq),
            lambda z, p: _conv_prompt(z, zero_cbuf, p, b_p, seq),
            attend_p, BF16, last)
        xs, ss = _mixer_layer(
            xs, p, table_s,
            lambda z, p, l=l: _lru_sample(z, l, lru_conv_t, state_lru_h, p, b_s, dec_seq),
            lambda z, p, l=l: _conv_sample(z, l, conv_t, p, b_s, dec_seq),
            attend_s, F32, last)
        st_p.append(sp)
        st_s.append(ss)

    def stack(states, k, shape):
        return jnp.stack([s[k].reshape(shape) for s in states])

    r = mla_kv_norm.shape[1]
    return (xp.reshape(b_p, seq, d_model), xs.reshape(b_s, dec_seq, d_model),
            stack(st_p, 0, (b_p, seq, r)), stack(st_p, 1, (b_p, seq, QK_ROPE)), stack(st_p, 2, (b_p, lw)),
            stack(st_p, 3, (b_p, LRU_CONV - 1, lw)), stack(st_p, 4, (b_p, CONV_K - 1, cw)),
            stack(st_s, 0, (b_s, dec_seq, r)), stack(st_s, 1, (b_s, dec_seq, QK_ROPE)), stack(st_s, 2, (b_s, lw)),
            stack(st_s, 3, (b_s, LRU_CONV - 1, lw)), stack(st_s, 4, (b_s, CONV_K - 1, cw)))
```

```python
import functools

import jax
import jax.numpy as jnp
from jax import lax
from jax.experimental import pallas as pl
from jax.experimental.pallas import tpu as pltpu

F32 = jnp.float32
BF16 = jnp.bfloat16

LRU_HEADS = 8
LRU_CONV = 4
LRU_C = 8.0
MLA_HEADS = 8
QK_NOPE = 128
QK_ROPE = 64
ROPE_THETA = 10000.0
ATTN_SCALE = (QK_NOPE + QK_ROPE) ** -0.5
PAGE_SIZE = 128
CONV_K = 31
RMS_EPS = 1e-6
LN_EPS = 1e-5

LANES = 128
SUBLANES = 8
VMEM_LIMIT_BYTES = 56 * 1024 * 1024

ROW_TILE = 512
IN_ROW_TILE = 1024
IN_COL_TILE = 1536
SEQ_TILE = 256
LRU_SAMPLE_SEQS = 128
CONV_SAMPLE_SEQS = 32
Q_TILE = 128
KV_TILE = 512
PAGES_PER_CHUNK = 8
PAGES_PER_GROUP = 2
PAGE_SLOTS = 3


def _params(*sem):
    return pltpu.CompilerParams(dimension_semantics=sem, vmem_limit_bytes=VMEM_LIMIT_BYTES)


def _dot(a, b):
    return jnp.dot(a, b, preferred_element_type=F32)


def _dot_nt(a, b):
    return lax.dot_general(a, b, (((1,), (1,)), ((), ())), preferred_element_type=F32)


def _rms(x, g):
    return x * lax.rsqrt(jnp.mean(x * x, axis=-1, keepdims=True) + RMS_EPS) * g


def _silu(x):
    return x * jax.nn.sigmoid(x)


def _seq_block(batch, preferred):
    if batch <= preferred:
        return batch
    assert batch % preferred == 0
    return preferred


def _in_proj_kernel(x_ref, g_ref, w_ref, wkr_ref, z_ref, zkr_ref, hn_ref):
    @pl.when(pl.program_id(1) == 0)
    def _():
        hn = _rms(x_ref[...], g_ref[...]).astype(BF16)
        hn_ref[...] = hn
        zkr_ref[...] = _dot(hn, wkr_ref[...])

    z_ref[...] = _dot(hn_ref[...], w_ref[...])


def _in_proj(x, g, w_main, w_kr):
    n, d = x.shape
    cols = w_main.shape[1]
    tm = min(IN_ROW_TILE, n)
    tn = IN_COL_TILE
    return pl.pallas_call(
        _in_proj_kernel,
        grid=(n // tm, cols // tn),
        in_specs=[
            pl.BlockSpec((tm, d), lambda i, j: (i, 0)),
            pl.BlockSpec((1, d), lambda i, j: (0, 0)),
            pl.BlockSpec((d, tn), lambda i, j: (0, j)),
            pl.BlockSpec((d, LANES), lambda i, j: (0, 0)),
        ],
        out_specs=[
            pl.BlockSpec((tm, tn), lambda i, j: (i, j)),
            pl.BlockSpec((tm, LANES), lambda i, j: (i, 0)),
        ],
        out_shape=[jax.ShapeDtypeStruct((n, cols), F32), jax.ShapeDtypeStruct((n, LANES), F32)],
        scratch_shapes=[pltpu.VMEM((tm, d), BF16)],
        compiler_params=_params("parallel", "arbitrary"),
        name="in_proj",
    )(x, g, w_main, w_kr)


def _lru_gates(xc, wa_ref, ba_ref, wx_ref, bx_ref, lam_ref):
    xcb = xc.astype(BF16)
    r = jax.nn.sigmoid(_dot(xcb, wa_ref[...]) + ba_ref[...])
    i = jax.nn.sigmoid(_dot(xcb, wx_ref[...]) + bx_ref[...])
    softplus_neg_lam = jnp.log(1.0 + jnp.exp(-lam_ref[...]))
    a = jnp.exp(-LRU_C * r * softplus_neg_lam)
    return a, jnp.sqrt(1.0 - a * a) * i * xc


def _lru_kernel(zx_ref, zg_ref, buf_ref, h0_ref, cw_ref, cb_ref, wa_ref, ba_ref, wx_ref, bx_ref, lam_ref,
                out_ref, hlast_ref, nbuf_ref, xp_ref, h_ref, a_ref, u_ref, hs_ref, *, tl):
    pad = SUBLANES - (LRU_CONV - 1)

    @pl.when(pl.program_id(1) == 0)
    def _():
        xp_ref[pad:SUBLANES, :] = buf_ref[...]
        h_ref[...] = h0_ref[...]

    x = zx_ref[...]
    xp_ref[SUBLANES:SUBLANES + tl, :] = x
    xc = cb_ref[...] + cw_ref[LRU_CONV - 1:LRU_CONV, :] * x
    for k in range(LRU_CONV - 1):
        xc = xc + cw_ref[k:k + 1, :] * xp_ref[pad + k:pad + k + tl, :]
    tail = xp_ref[pad + tl:SUBLANES + tl, :]
    xp_ref[pad:SUBLANES, :] = tail
    nbuf_ref[...] = tail

    a, u = _lru_gates(xc, wa_ref, ba_ref, wx_ref, bx_ref, lam_ref)
    a_ref[...] = a
    u_ref[...] = u

    def step(t, h):
        h = a_ref[pl.ds(t, 1), :] * h + u_ref[pl.ds(t, 1), :]
        hs_ref[pl.ds(t, 1), :] = h
        return h

    h = lax.fori_loop(0, tl, step, h_ref[...], unroll=SUBLANES)
    h_ref[...] = h
    hlast_ref[...] = h
    out_ref[...] = hs_ref[...] * _silu(zg_ref[...])


def _lru_prompt(z, buf, h0, p, batch, seq):
    w = p['lru_cw'].shape[1]
    tl = min(SEQ_TILE, seq)
    nc = seq // tl
    row = lambda b, c: (b * nc + c, 0)
    vec = pl.BlockSpec((1, w), lambda b, c: (0, 0))
    mat = pl.BlockSpec((w, w), lambda b, c: (0, 0))
    out_a, h_last, nbuf = pl.pallas_call(
        functools.partial(_lru_kernel, tl=tl),
        grid=(batch, nc),
        in_specs=[
            pl.BlockSpec((tl, w), row),
            pl.BlockSpec((tl, w), lambda b, c: (b * nc + c, 1)),
            pl.BlockSpec((None, LRU_CONV - 1, w), lambda b, c: (b, 0, 0)),
            pl.BlockSpec((None, 1, w), lambda b, c: (b, 0, 0)),
            pl.BlockSpec((LRU_CONV, w), lambda b, c: (0, 0)),
            vec, mat, vec, mat, vec, vec,
        ],
        out_specs=[
            pl.BlockSpec((tl, w), row),
            pl.BlockSpec((None, 1, w), lambda b, c: (b, 0, 0)),
            pl.BlockSpec((None, LRU_CONV - 1, w), lambda b, c: (b, 0, 0)),
        ],
        out_shape=[
            jax.ShapeDtypeStruct((batch * seq, w), F32),
            jax.ShapeDtypeStruct((batch, 1, w), F32),
            jax.ShapeDtypeStruct((batch, LRU_CONV - 1, w), F32),
        ],
        scratch_shapes=[
            pltpu.VMEM((tl + SUBLANES, w), F32),
            pltpu.VMEM((1, w), F32),
            pltpu.VMEM((tl, w), F32),
            pltpu.VMEM((tl, w), F32),
            pltpu.VMEM((tl, w), F32),
        ],
        compiler_params=_params("parallel", "arbitrary"),
        name="lru_prompt",
    )(z, z, buf, h0, p['lru_cw'], p['lru_cb'], p['lru_wa'], p['lru_ba'], p['lru_wx'], p['lru_bx'], p['lru_lam'])
    return out_a, h_last[:, 0], nbuf


def _token_slab(refs, k, bb, t):
    return jnp.concatenate([ref[pl.ds(k, bb, stride=t), :] for ref in refs], axis=-1)


def _store_token_slab(stage_ref, k, bb, t, val):
    for c in range(stage_ref.shape[0]):
        stage_ref[c, pl.ds(k, bb, stride=t), :] = val[:, c * LANES:(c + 1) * LANES]


def _unstage(stage_ref, out_ref):
    for c in range(stage_ref.shape[0]):
        out_ref[:, c * LANES:(c + 1) * LANES] = stage_ref[c]


def _lane_tile_specs(rows, width, col0):
    nl = width // LANES
    return [pl.BlockSpec((rows, LANES), lambda i, c=c: (i, col0 * nl + c)) for c in range(nl)]


def _lru_sample_kernel(*refs, t, bb, nl):
    zx_refs, zg_refs = refs[:nl], refs[nl:2 * nl]
    (buf_ref, h0_ref, cw_ref, cb_ref, wa_ref, ba_ref, wx_ref, bx_ref, lam_ref,
     out_ref, hlast_ref, nbuf_ref, stage_ref) = refs[2 * nl:]
    xp = [buf_ref[k] for k in range(LRU_CONV - 1)] + [_token_slab(zx_refs, k, bb, t) for k in range(t)]
    h = h0_ref[...]
    for k in range(t):
        xc = cb_ref[...] + cw_ref[0:1, :] * xp[k]
        for j in range(1, LRU_CONV):
            xc = xc + cw_ref[j:j + 1, :] * xp[k + j]
        a, u = _lru_gates(xc, wa_ref, ba_ref, wx_ref, bx_ref, lam_ref)
        h = a * h + u
        _store_token_slab(stage_ref, k, bb, t, h * _silu(_token_slab(zg_refs, k, bb, t)))
    _unstage(stage_ref, out_ref)
    hlast_ref[...] = h
    for j in range(LRU_CONV - 1):
        nbuf_ref[j] = xp[t + j]


def _lru_sample(z, layer, buf_t, h0, p, batch, t):
    w = p['lru_cw'].shape[1]
    nl = w // LANES
    bb = _seq_block(batch, LRU_SAMPLE_SEQS)
    vec = pl.BlockSpec((1, w), lambda i: (0, 0))
    mat = pl.BlockSpec((w, w), lambda i: (0, 0))
    out_a, h_last, nbuf = pl.pallas_call(
        functools.partial(_lru_sample_kernel, t=t, bb=bb, nl=nl),
        grid=(batch // bb,),
        in_specs=[
            *_lane_tile_specs(bb * t, w, 0),
            *_lane_tile_specs(bb * t, w, 1),
            pl.BlockSpec((None, LRU_CONV - 1, bb, w), lambda i: (layer, 0, i, 0)),
            pl.BlockSpec((None, bb, w), lambda i: (layer, i, 0)),
            pl.BlockSpec((LRU_CONV, w), lambda i: (0, 0)),
            vec, mat, vec, mat, vec, vec,
        ],
        out_specs=[
            pl.BlockSpec((bb * t, w), lambda i: (i, 0)),
            pl.BlockSpec((bb, w), lambda i: (i, 0)),
            pl.BlockSpec((LRU_CONV - 1, bb, w), lambda i: (0, i, 0)),
        ],
        out_shape=[
            jax.ShapeDtypeStruct((batch * t, w), F32),
            jax.ShapeDtypeStruct((batch, w), F32),
            jax.ShapeDtypeStruct((LRU_CONV - 1, batch, w), F32),
        ],
        scratch_shapes=[pltpu.VMEM((nl, bb * t, LANES), F32)],
        compiler_params=_params("parallel"),
        name="lru_sample",
    )(*([z] * (2 * nl)), buf_t, h0, p['lru_cw'], p['lru_cb'], p['lru_wa'], p['lru_ba'], p['lru_wx'], p['lru_bx'],
      p['lru_lam'])
    return out_a, h_last, jnp.transpose(nbuf, (1, 0, 2))


CONV_HIST = CONV_K - 1
CONV_PAD = 32


def _ln_silu_gate(y, lng_ref, lnb_ref, gate):
    mu = jnp.mean(y, axis=-1, keepdims=True)
    yc = y - mu
    yn = yc * lax.rsqrt(jnp.mean(yc * yc, axis=-1, keepdims=True) + LN_EPS) * lng_ref[...] + lnb_ref[...]
    return _silu(yn) * _silu(gate)


def _conv_kernel(zv_ref, zglu_ref, zg_ref, buf_ref, w_ref, b_ref, lng_ref, lnb_ref,
                 out_ref, nbuf_ref, s_ref, *, tl):
    base = CONV_PAD - CONV_HIST

    @pl.when(pl.program_id(1) == 0)
    def _():
        s_ref[base:CONV_PAD, :] = buf_ref[...]

    s_ref[CONV_PAD:CONV_PAD + tl, :] = zv_ref[...] * jax.nn.sigmoid(zglu_ref[...])
    y = b_ref[...] + w_ref[0:1, :] * s_ref[base:base + tl, :]
    for k in range(1, CONV_K):
        y = y + w_ref[k:k + 1, :] * s_ref[base + k:base + k + tl, :]
    tail = s_ref[base + tl:CONV_PAD + tl, :]
    s_ref[base:CONV_PAD, :] = tail
    nbuf_ref[...] = tail
    out_ref[...] = _ln_silu_gate(y, lng_ref, lnb_ref, zg_ref[...])


def _conv_prompt(z, buf, p, batch, seq):
    cw = p['dw_w'].shape[1]
    col0 = p['conv_col0']
    tl = min(SEQ_TILE, seq)
    nc = seq // tl
    vec = pl.BlockSpec((1, cw), lambda b_, c: (0, 0))
    zspec = lambda off: pl.BlockSpec((tl, cw), lambda b_, c: (b_ * nc + c, col0 + off))
    return pl.pallas_call(
        functools.partial(_conv_kernel, tl=tl),
        grid=(batch, nc),
        in_specs=[
            zspec(0), zspec(1), zspec(2),
            pl.BlockSpec((None, CONV_HIST, cw), lambda b_, c: (b_, 0, 0)),
            pl.BlockSpec((CONV_K, cw), lambda b_, c: (0, 0)),
            vec, vec, vec,
        ],
        out_specs=[
            pl.BlockSpec((tl, cw), lambda b_, c: (b_ * nc + c, 0)),
            pl.BlockSpec((None, CONV_HIST, cw), lambda b_, c: (b_, 0, 0)),
        ],
        out_shape=[
            jax.ShapeDtypeStruct((batch * seq, cw), F32),
            jax.ShapeDtypeStruct((batch, CONV_HIST, cw), F32),
        ],
        scratch_shapes=[pltpu.VMEM((tl + CONV_PAD, cw), F32)],
        compiler_params=_params("parallel", "arbitrary"),
        name="conv_prompt",
    )(z, z, z, buf, p['dw_w'], p['dw_b'], p['ln_g'], p['ln_b'])


def _conv_sample_kernel(*refs, t, bb, nl):
    zv_refs, zglu_refs, zg_refs = refs[:nl], refs[nl:2 * nl], refs[2 * nl:3 * nl]
    buf_ref, w_ref, b_ref, lng_ref, lnb_ref, out_ref, nbuf_ref, u_ref, stage_ref = refs[3 * nl:]
    for k in range(t):
        u_ref[k] = _token_slab(zv_refs, k, bb, t) * jax.nn.sigmoid(_token_slab(zglu_refs, k, bb, t))

    def tap(j):
        return buf_ref[j] if j < CONV_HIST else u_ref[j - CONV_HIST]

    for k in range(t):
        y = b_ref[...] + w_ref[0:1, :] * tap(k)
        for j in range(1, CONV_K):
            y = y + w_ref[j:j + 1, :] * tap(k + j)
        _store_token_slab(stage_ref, k, bb, t, _ln_silu_gate(y, lng_ref, lnb_ref, _token_slab(zg_refs, k, bb, t)))
    _unstage(stage_ref, out_ref)
    for j in range(CONV_HIST):
        nbuf_ref[j] = tap(t + j)


def _conv_sample(z, layer, buf_t, p, batch, t):
    cw = p['dw_w'].shape[1]
    nl = cw // LANES
    col0 = p['conv_col0']
    bb = _seq_block(batch, CONV_SAMPLE_SEQS)
    vec = pl.BlockSpec((1, cw), lambda i: (0, 0))
    out_c, nbuf = pl.pallas_call(
        functools.partial(_conv_sample_kernel, t=t, bb=bb, nl=nl),
        grid=(batch // bb,),
        in_specs=[
            *_lane_tile_specs(bb * t, cw, col0),
            *_lane_tile_specs(bb * t, cw, col0 + 1),
            *_lane_tile_specs(bb * t, cw, col0 + 2),
            pl.BlockSpec((None, CONV_HIST, bb, cw), lambda i: (layer, 0, i, 0)),
            pl.BlockSpec((CONV_K, cw), lambda i: (0, 0)),
            vec, vec, vec,
        ],
        out_specs=[
            pl.BlockSpec((bb * t, cw), lambda i: (i, 0)),
            pl.BlockSpec((CONV_HIST, bb, cw), lambda i: (0, i, 0)),
        ],
        out_shape=[
            jax.ShapeDtypeStruct((batch * t, cw), F32),
            jax.ShapeDtypeStruct((CONV_HIST, batch, cw), F32),
        ],
        scratch_shapes=[pltpu.VMEM((t, bb, cw), F32), pltpu.VMEM((nl, bb * t, LANES), F32)],
        compiler_params=_params("parallel"),
        name="conv_sample",
    )(*([z] * (3 * nl)), buf_t, p['dw_w'], p['dw_b'], p['ln_g'], p['ln_b'])
    return out_c, jnp.transpose(nbuf, (1, 0, 2))


def _rope128(v, table):
    a = v * table
    return a + pltpu.roll(a, QK_ROPE, 1)


def _mla_proj_kernel(zq_ref, zkv_ref, zkr_ref, qn_ref, kvn_ref, wn_ref, wr_ref, wuk_ref, tab_ref,
                     qa_ref, qr_ref, ckv_ref, ckvb_ref, krope_ref, krp_ref):
    tab = tab_ref[...]
    cq = _rms(zq_ref[...], qn_ref[...]).astype(BF16)
    q_nope = _dot(cq, wn_ref[...])
    q_rope = _dot(cq, wr_ref[...])
    for h in range(MLA_HEADS):
        qh = q_nope[:, h * QK_NOPE:(h + 1) * QK_NOPE].astype(BF16)
        qa_ref[h] = _dot(qh, wuk_ref[h]).astype(qa_ref.dtype)
        qr_ref[h] = _rope128(q_rope[:, h * LANES:(h + 1) * LANES], tab).astype(qr_ref.dtype)
    ckv = _rms(zkv_ref[...], kvn_ref[...])
    ckv_ref[...] = ckv
    ckvb_ref[...] = ckv.astype(BF16)
    kro = _rope128(zkr_ref[...], tab)
    krope_ref[...] = kro[:, :QK_ROPE]
    lane = lax.broadcasted_iota(jnp.int32, kro.shape, 1)
    krp_ref[...] = jnp.where(lane < QK_ROPE, kro, 0.0).astype(BF16)


def _mla_proj(z, zkr, p, table, q_dtype):
    n = z.shape[0]
    r = p['qn'].shape[1]
    tm = min(ROW_TILE, n)
    nt = table.shape[0] // tm
    wn, wr, wuk = p['wn'], p['wr'], p['wuk']
    full = lambda shape: pl.BlockSpec(shape, lambda i: (0,) * len(shape))
    return pl.pallas_call(
        _mla_proj_kernel,
        grid=(n // tm,),
        in_specs=[
            pl.BlockSpec((tm, r), lambda i: (i, 2)),
            pl.BlockSpec((tm, r), lambda i: (i, 3)),
            pl.BlockSpec((tm, LANES), lambda i: (i, 0)),
            full((1, r)), full((1, r)),
            full(wn.shape), full(wr.shape), full(wuk.shape),
            pl.BlockSpec((tm, LANES), lambda i: (i % nt, 0)),
        ],
        out_specs=[
            pl.BlockSpec((MLA_HEADS, tm, r), lambda i: (0, i, 0)),
            pl.BlockSpec((MLA_HEADS, tm, LANES), lambda i: (0, i, 0)),
            pl.BlockSpec((tm, r), lambda i: (i, 0)),
            pl.BlockSpec((tm, r), lambda i: (i, 0)),
            pl.BlockSpec((tm, QK_ROPE), lambda i: (i, 0)),
            pl.BlockSpec((tm, LANES), lambda i: (i, 0)),
        ],
        out_shape=[
            jax.ShapeDtypeStruct((MLA_HEADS, n, r), q_dtype),
            jax.ShapeDtypeStruct((MLA_HEADS, n, LANES), q_dtype),
            jax.ShapeDtypeStruct((n, r), F32),
            jax.ShapeDtypeStruct((n, r), BF16),
            jax.ShapeDtypeStruct((n, QK_ROPE), F32),
            jax.ShapeDtypeStruct((n, LANES), BF16),
        ],
        compiler_params=_params("parallel"),
        name="mla_proj",
    )(z, z, zkr, p['qn'], p['kvn'], wn, wr, wuk, table)


def _local_softmax(s, v):
    m = jnp.max(s, axis=-1, keepdims=True)
    p = jnp.exp(s - m)
    return m, jnp.sum(p, axis=-1, keepdims=True), _dot(p.astype(BF16), v)


def _merge_softmax(parts):
    m = parts[0][0]
    for part in parts[1:]:
        m = jnp.maximum(m, part[0])
    l = acc = None
    for m_g, l_g, acc_g in parts:
        w = jnp.exp(m_g - m)
        l = l_g * w if l is None else l + l_g * w
        acc = acc_g * w if acc is None else acc + acc_g * w
    return m, l, acc


def _prompt_attn_kernel(qi_ref, kj_ref, qa_ref, qr_ref, kc_ref, kr_ref, v_ref, wuv_ref, zg_ref,
                        out_ref, s_ref, m_ref, l_ref, acc_ref, *, tq, tk, n_tiles):
    n = pl.program_id(1)
    rows = MLA_HEADS * tq

    def scores():
        qa = qa_ref[...].reshape(rows, qa_ref.shape[-1])
        qr = qr_ref[...].reshape(rows, LANES)
        return (_dot_nt(qa, kc_ref[...]) + _dot_nt(qr, kr_ref[...])) * ATTN_SCALE

    prev = jnp.maximum(n - 1, 0)
    i = qi_ref[prev]
    j = kj_ref[prev]

    def absorb():
        q_pos = i * tq + (lax.broadcasted_iota(jnp.int32, (rows, 1), 0) & (tq - 1))
        k_pos = j * tk + lax.broadcasted_iota(jnp.int32, (1, tk), 1)
        s = jnp.where(k_pos <= q_pos, s_ref[...], -jnp.inf)
        restart = j == 0
        m_old = jnp.where(restart, -jnp.inf, m_ref[...])
        m_new = jnp.maximum(m_old, jnp.max(s, axis=-1, keepdims=True))
        corr = jnp.exp(m_old - m_new)
        p = jnp.exp(s - m_new)
        l_ref[...] = jnp.where(restart, 0.0, l_ref[...]) * corr + jnp.sum(p, axis=-1, keepdims=True)
        acc_ref[...] = jnp.where(restart, 0.0, acc_ref[...]) * corr + _dot(p.astype(BF16), v_ref[...])
        m_ref[...] = m_new

    @pl.when(n == 0)
    def _():
        m_ref[...] = jnp.full(m_ref.shape, -jnp.inf, F32)
        l_ref[...] = jnp.zeros(l_ref.shape, F32)
        acc_ref[...] = jnp.zeros(acc_ref.shape, F32)
        s_ref[...] = scores()

    @pl.when(jnp.logical_and(n > 0, n < n_tiles))
    def _():
        s_next = scores()
        absorb()
        s_ref[...] = s_next

    @pl.when(n == n_tiles)
    def _():
        absorb()

    @pl.when(jnp.logical_and(n > 0, j == (i * tq + tq - 1) // tk))
    def _():
        ob = (acc_ref[...] / l_ref[...]).astype(BF16)
        heads = [_dot(ob[h * tq:(h + 1) * tq], wuv_ref[h]) for h in range(MLA_HEADS)]
        out_ref[...] = jnp.concatenate(heads, axis=-1) * _silu(zg_ref[...])


def _causal_tiles(seq, tq, tk):
    tiles = [(i, j) for i in range(seq // tq) for j in range((i * tq + tq - 1) // tk + 1)]
    tiles.append(tiles[-1])
    return jnp.array([t[0] for t in tiles], jnp.int32), jnp.array([t[1] for t in tiles], jnp.int32), len(tiles) - 1


def _prompt_attention(qa, qr, kc, kr, wuv, z, batch, seq):
    r = qa.shape[-1]
    tq = min(Q_TILE, seq)
    tk = min(KV_TILE, seq)
    assert tq & (tq - 1) == 0
    nq, nk = seq // tq, seq // tk
    width = wuv.shape[0] * wuv.shape[2]
    qi, kj, n_tiles = _causal_tiles(seq, tq, tk)
    lag = lambda n: jnp.maximum(n - 1, 0)
    grid_spec = pltpu.PrefetchScalarGridSpec(
        num_scalar_prefetch=2,
        grid=(batch, n_tiles + 1),
        in_specs=[
            pl.BlockSpec((MLA_HEADS, tq, r), lambda b, n, qi, kj: (0, b * nq + qi[n], 0)),
            pl.BlockSpec((MLA_HEADS, tq, LANES), lambda b, n, qi, kj: (0, b * nq + qi[n], 0)),
            pl.BlockSpec((tk, r), lambda b, n, qi, kj: (b * nk + kj[n], 0)),
            pl.BlockSpec((tk, LANES), lambda b, n, qi, kj: (b * nk + kj[n], 0)),
            pl.BlockSpec((tk, r), lambda b, n, qi, kj: (b * nk + kj[lag(n)], 0)),
            pl.BlockSpec(wuv.shape, lambda b, n, qi, kj: (0, 0, 0)),
            pl.BlockSpec((tq, width), lambda b, n, qi, kj: (b * nq + qi[lag(n)], 2)),
        ],
        out_specs=pl.BlockSpec((tq, width), lambda b, n, qi, kj: (b * nq + qi[lag(n)], 0)),
        scratch_shapes=[
            pltpu.VMEM((MLA_HEADS * tq, tk), F32),
            pltpu.VMEM((MLA_HEADS * tq, 1), F32),
            pltpu.VMEM((MLA_HEADS * tq, 1), F32),
            pltpu.VMEM((MLA_HEADS * tq, r), F32),
        ],
    )
    return pl.pallas_call(
        functools.partial(_prompt_attn_kernel, tq=tq, tk=tk, n_tiles=n_tiles),
        grid_spec=grid_spec,
        out_shape=jax.ShapeDtypeStruct((batch * seq, width), F32),
        compiler_params=_params("parallel", "arbitrary"),
        name="prompt_attention",
    )(qi, kj, qa, qr, kc, kr, kc, wuv, z)


def _sample_attn_kernel(pt_ref, qa_ref, qr_ref, kcn_ref, krn_ref, wuv_ref, zg_ref, lat_hbm, rope_hbm,
                        out_ref, lat_buf, rope_buf, sem, kb_buf, m_ref, l_ref, acc_ref,
                        *, layer, t, n_pages, chunk, group):
    b = pl.program_id(0)
    n_chunks = n_pages // chunk
    total = pl.num_programs(0) * n_chunks
    first = b * n_chunks
    rows = MLA_HEADS * t
    r = qa_ref.shape[-1]

    def page_copies(n):
        slot = n % PAGE_SLOTS
        copies = []
        for g in range(chunk):
            page = pt_ref[n * chunk + g]
            copies.append(pltpu.make_async_copy(lat_hbm.at[layer, page], lat_buf.at[slot, g], sem.at[0, slot]))
            copies.append(pltpu.make_async_copy(rope_hbm.at[layer, page], rope_buf.at[slot, g], sem.at[1, slot]))
        return copies

    def fetch(n):
        @pl.when(n < total)
        def _():
            for cp in page_copies(n):
                cp.start()

    def consume(n):
        fetch(n + PAGE_SLOTS - 1)
        for cp in page_copies(n):
            cp.wait()

    @pl.when(b == 0)
    def _():
        for n in range(PAGE_SLOTS - 1):
            fetch(n)

    qa = qa_ref[...].reshape(rows, r).astype(BF16)
    qr = qr_ref[...].reshape(rows, LANES)[:, :QK_ROPE].astype(BF16)

    def scores(n, kb_slot):
        slot = n % PAGE_SLOTS
        kb = lat_buf[slot].reshape(chunk * PAGE_SIZE, r).astype(BF16)
        kb_buf[kb_slot] = kb
        s_rope = jnp.concatenate([_dot(qr, rope_buf[slot, g].astype(BF16)) for g in range(chunk)], axis=-1)
        return (_dot_nt(qa, kb) + s_rope) * ATTN_SCALE

    def absorb(s, kb_slot):
        parts = [(m_ref[...], l_ref[...], acc_ref[...])]
        for k0 in range(0, chunk * PAGE_SIZE, group * PAGE_SIZE):
            k1 = k0 + group * PAGE_SIZE
            parts.append(_local_softmax(s[:, k0:k1], kb_buf[kb_slot, k0:k1, :]))
        m, l, acc = _merge_softmax(parts)
        m_ref[...] = m
        l_ref[...] = l
        acc_ref[...] = acc

    consume(first)
    s = scores(first, 0)

    kc = jnp.concatenate([kcn_ref[...], jnp.zeros((LANES - t, r), F32)], axis=0).astype(BF16)
    kr = jnp.concatenate([krn_ref[...], jnp.zeros((LANES - t, QK_ROPE), F32)], axis=0).astype(BF16)
    s_tok = (_dot_nt(qa, kc) + _dot_nt(qr, kr)) * ATTN_SCALE
    q_tok = lax.broadcasted_iota(jnp.int32, (rows, 1), 0) % t
    k_tok = lax.broadcasted_iota(jnp.int32, (1, LANES), 1)
    m0, l0, acc0 = _local_softmax(jnp.where(k_tok <= q_tok, s_tok, -jnp.inf), kc)
    m_ref[...] = m0
    l_ref[...] = l0
    acc_ref[...] = acc0

    for c in range(1, n_chunks):
        consume(first + c)
        s_next = scores(first + c, c % 2)
        absorb(s, (c - 1) % 2)
        s = s_next
    absorb(s, (n_chunks - 1) % 2)

    ob = (acc_ref[...] / l_ref[...]).astype(BF16)
    heads = [_dot(ob, wuv_ref[h])[h * t:(h + 1) * t] for h in range(MLA_HEADS)]
    out_ref[...] = jnp.concatenate(heads, axis=-1) * _silu(zg_ref[...])


def _sample_attention(qa, qr, ckv, krope, cache_lat, cache_rope_t, layer, page_table, wuv, z, batch, t):
    assert t == SUBLANES
    r = qa.shape[-1]
    n_pages = page_table.shape[1]
    chunk = min(PAGES_PER_CHUNK, n_pages)
    group = min(PAGES_PER_GROUP, chunk)
    assert n_pages % chunk == 0 and chunk % group == 0
    width = wuv.shape[0] * wuv.shape[2]
    grid_spec = pltpu.PrefetchScalarGridSpec(
        num_scalar_prefetch=1,
        grid=(batch,),
        in_specs=[
            pl.BlockSpec((MLA_HEADS, t, r), lambda b, pt_ref: (0, b, 0)),
            pl.BlockSpec((MLA_HEADS, t, LANES), lambda b, pt_ref: (0, b, 0)),
            pl.BlockSpec((t, r), lambda b, pt_ref: (b, 0)),
            pl.BlockSpec((t, QK_ROPE), lambda b, pt_ref: (b, 0)),
            pl.BlockSpec(wuv.shape, lambda b, pt_ref: (0, 0, 0)),
            pl.BlockSpec((t, width), lambda b, pt_ref: (b, 2)),
            pl.BlockSpec(memory_space=pl.ANY),
            pl.BlockSpec(memory_space=pl.ANY),
        ],
        out_specs=pl.BlockSpec((t, width), lambda b, pt_ref: (b, 0)),
        scratch_shapes=[
            pltpu.VMEM((PAGE_SLOTS, chunk, PAGE_SIZE, r), F32),
            pltpu.VMEM((PAGE_SLOTS, chunk, QK_ROPE, PAGE_SIZE), F32),
            pltpu.SemaphoreType.DMA((2, PAGE_SLOTS)),
            pltpu.VMEM((2, chunk * PAGE_SIZE, r), BF16),
            pltpu.VMEM((MLA_HEADS * t, 1), F32),
            pltpu.VMEM((MLA_HEADS * t, 1), F32),
            pltpu.VMEM((MLA_HEADS * t, r), F32),
        ],
    )
    return pl.pallas_call(
        functools.partial(_sample_attn_kernel, layer=layer, t=t, n_pages=n_pages, chunk=chunk, group=group),
        grid_spec=grid_spec,
        out_shape=jax.ShapeDtypeStruct((batch * t, width), F32),
        compiler_params=_params("arbitrary"),
        name="sample_attention",
    )(page_table.reshape(-1), qa, qr, ckv, krope, wuv, z, cache_lat, cache_rope_t)


def _out_proj_kernel(a_ref, b_ref, c_ref, x_ref, wa_ref, wb_ref, wc_ref, g_ref, y_ref, *, final_norm):
    y = x_ref[...] + _dot(a_ref[...].astype(BF16), wa_ref[...])
    y = y + _dot(b_ref[...].astype(BF16), wb_ref[...])
    y = y + _dot(c_ref[...].astype(BF16), wc_ref[...])
    if final_norm:
        y = _rms(y, g_ref[...])
    y_ref[...] = y


def _out_proj(out_a, out_b, out_c, x, wa, wb, wc, g, final_norm):
    n, d = x.shape
    tm = min(ROW_TILE // 2, n)
    rows = lambda width: pl.BlockSpec((tm, width), lambda i: (i, 0))
    full = lambda arr: pl.BlockSpec(arr.shape, lambda i: (0, 0))
    return pl.pallas_call(
        functools.partial(_out_proj_kernel, final_norm=final_norm),
        grid=(n // tm,),
        in_specs=[rows(out_a.shape[1]), rows(out_b.shape[1]), rows(out_c.shape[1]), rows(d),
                  full(wa), full(wb), full(wc), full(g)],
        out_specs=rows(d),
        out_shape=jax.ShapeDtypeStruct((n, d), F32),
        compiler_params=_params("parallel"),
        name="out_proj",
    )(out_a, out_b, out_c, x, wa, wb, wc, g)


def _rope_table(pos, rows):
    half = QK_ROPE // 2
    freqs = ROPE_THETA ** (-jnp.arange(half, dtype=F32) / half)
    ang = pos.astype(F32)[:, None] * freqs[None, :]
    cos, sin = jnp.cos(ang), jnp.sin(ang)
    table = jnp.concatenate([cos, cos, -sin, sin], axis=-1)
    reps = max(1, rows // table.shape[0])
    return jnp.tile(table, (reps, 1))


def _block_diag(w):
    h, d, _ = w.shape
    eye = jnp.eye(h, dtype=w.dtype)
    return (eye[:, None, :, None] * w[:, :, None, :]).reshape(h * d, h * d)


def _swap_halves(w):
    half = w.shape[-1] // 2
    return jnp.concatenate([w[..., half:], w[..., :half]], axis=-1)


def _layer_weights(l, norm_g, w_in, w_out, lru_conv_w, lru_conv_b, lru_w_a, lru_b_a, lru_w_x, lru_b_x, lru_lambda,
                   mla_q_norm, mla_kv_norm, mla_w_uq, mla_w_uk, mla_w_uv, conv_dw_w, conv_dw_b, conv_ln_g,
                   conv_ln_b):
    lw = lru_conv_w.shape[2]
    q_rank = mla_q_norm.shape[1]
    kv_rank = mla_kv_norm.shape[1]
    mla_w = mla_w_uv.shape[2] * mla_w_uv.shape[3]
    cw = conv_dw_w.shape[2]
    sizes = (lw, lw, q_rank, kv_rank, QK_ROPE, mla_w, cw, cw, cw)
    offs = [0]
    for s in sizes:
        offs.append(offs[-1] + s)
    w = w_in[l]
    col = lambda k: w[:, offs[k]:offs[k + 1]]
    w_main = jnp.concatenate([col(0), col(1), col(2), col(3), col(5), col(6), col(7), col(8)], axis=1).astype(BF16)
    w_kr = jnp.concatenate([col(4), _swap_halves(col(4))], axis=1).astype(BF16)
    wuq = mla_w_uq[l]
    wn = wuq[:, :, :QK_NOPE].reshape(q_rank, -1).astype(BF16)
    wrope = wuq[:, :, QK_NOPE:]
    wr = jnp.concatenate([wrope, _swap_halves(wrope)], axis=-1).reshape(q_rank, -1).astype(BF16)
    wuk = jnp.transpose(mla_w_uk[l], (1, 2, 0)).astype(BF16)
    wuv = jnp.transpose(mla_w_uv[l], (1, 0, 2)).astype(BF16)
    wo = w_out[l].astype(BF16)
    row = lambda v: v[l][None, :]
    return dict(
        norm_g=row(norm_g), w_main=w_main, w_kr=w_kr,
        lru_cw=lru_conv_w[l], lru_cb=row(lru_conv_b),
        lru_wa=_block_diag(lru_w_a[l]).astype(BF16), lru_ba=row(lru_b_a),
        lru_wx=_block_diag(lru_w_x[l]).astype(BF16), lru_bx=row(lru_b_x), lru_lam=row(lru_lambda),
        qn=row(mla_q_norm), kvn=row(mla_kv_norm), wn=wn, wr=wr, wuk=wuk, wuv=wuv,
        dw_w=conv_dw_w[l], dw_b=row(conv_dw_b), ln_g=row(conv_ln_g), ln_b=row(conv_ln_b),
        wo_a=wo[:lw], wo_b=wo[lw:lw + mla_w], wo_c=wo[lw + mla_w:],
        conv_col0=(2 * lw + q_rank + kv_rank + mla_w) // cw,
    )


def _mixer_layer(x, p, table, lru_fn, conv_fn, attend, q_dtype, final_g):
    z, zkr = _in_proj(x, p['norm_g'], p['w_main'], p['w_kr'])
    out_a, h_last, lru_buf_new = lru_fn(z, p)
    out_c, conv_buf_new = conv_fn(z, p)
    qa, qr, ckv, ckv_bf, krope, krp = _mla_proj(z, zkr, p, table, q_dtype)
    out_b = attend(qa, qr, ckv, ckv_bf, krope, krp, p['wuv'], z)
    y = _out_proj(out_a, out_b, out_c, x, p['wo_a'], p['wo_b'], p['wo_c'],
                  final_g if final_g is not None else p['norm_g'], final_g is not None)
    return y, (ckv, krope, h_last, lru_buf_new, conv_buf_new)


def kernel(x_prompt, x_sample, cache_kv_latent, cache_k_rope, page_table, state_lru_h, state_lru_conv, state_conv, norm_g, w_in, w_out, lru_conv_w, lru_conv_b, lru_w_a, lru_b_a, lru_w_x, lru_b_x, lru_lambda, mla_q_norm, mla_kv_norm, mla_w_uq, mla_w_uk, mla_w_uv, conv_dw_w, conv_dw_b, conv_ln_g, conv_ln_b, final_norm_g):
    b_p, seq, d_model = x_prompt.shape
    b_s, dec_seq, _ = x_sample.shape
    depth = norm_g.shape[0]
    lw = lru_conv_w.shape[2]
    cw = conv_dw_w.shape[2]
    past_len = page_table.shape[1] * PAGE_SIZE
    n_p, n_s = b_p * seq, b_s * dec_seq
    table_p = _rope_table(jnp.arange(seq, dtype=F32), min(ROW_TILE, n_p))
    table_s = _rope_table(past_len + jnp.arange(dec_seq, dtype=F32), min(ROW_TILE, n_s))
    zero_h = jnp.zeros((b_p, 1, lw), F32)
    zero_lbuf = jnp.zeros((b_p, LRU_CONV - 1, lw), F32)
    zero_cbuf = jnp.zeros((b_p, CONV_K - 1, cw), F32)
    lru_conv_t = jnp.transpose(state_lru_conv, (0, 2, 1, 3))
    conv_t = jnp.transpose(state_conv, (0, 2, 1, 3))
    cache_rope_t = jnp.swapaxes(cache_k_rope, 2, 3)
    final_g = final_norm_g[None, :]

    xp = x_prompt.reshape(n_p, d_model)
    xs = x_sample.reshape(n_s, d_model)
    st_p, st_s = [], []
    for l in range(depth):
        p = _layer_weights(l, norm_g, w_in, w_out, lru_conv_w, lru_conv_b, lru_w_a, lru_b_a, lru_w_x, lru_b_x,
                           lru_lambda, mla_q_norm, mla_kv_norm, mla_w_uq, mla_w_uk, mla_w_uv, conv_dw_w,
                           conv_dw_b, conv_ln_g, conv_ln_b)
        last = final_g if l == depth - 1 else None

        def attend_p(qa, qr, ckv, ckv_bf, krope, krp, wuv, z):
            return _prompt_attention(qa, qr, ckv_bf, krp, wuv, z, b_p, seq)

        def attend_s(qa, qr, ckv, ckv_bf, krope, krp, wuv, z, l=l):
            return _sample_attention(qa, qr, ckv, krope, cache_kv_latent, cache_rope_t, l, page_table, wuv, z,
                                     b_s, dec_seq)

        xp, sp = _mixer_layer(
            xp, p, table_p,
            lambda z, p: _lru_prompt(z, zero_lbuf, zero_h, p, b_p, seq),
            lambda z, p: _conv_prompt(z, zero_cbuf, p, b_p, seq),
            attend_p, BF16, last)
        xs, ss = _mixer_layer(
            xs, p, table_s,
            lambda z, p, l=l: _lru_sample(z, l, lru_conv_t, state_lru_h, p, b_s, dec_seq),
            lambda z, p, l=l: _conv_sample(z, l, conv_t, p, b_s, dec_seq),
            attend_s, F32, last)
        st_p.append(sp)
        st_s.append(ss)

    def stack(states, k, shape):
        return jnp.stack([s[k].reshape(shape) for s in states])

    r = mla_kv_norm.shape[1]
    return (xp.reshape(b_p, seq, d_model), xs.reshape(b_s, dec_seq, d_model),
            stack(st_p, 0, (b_p, seq, r)), stack(st_p, 1, (b_p, seq, QK_ROPE)), stack(st_p, 2, (b_p, lw)),
            stack(st_p, 3, (b_p, LRU_CONV - 1, lw)), stack(st_p, 4, (b_p, CONV_K - 1, cw)),
            stack(st_s, 0, (b_s, dec_seq, r)), stack(st_s, 1, (b_s, dec_seq, QK_ROPE)), stack(st_s, 2, (b_s, lw)),
            stack(st_s, 3, (b_s, LRU_CONV - 1, lw)), stack(st_s, 4, (b_s, CONV_K - 1, cw)))
```

```python
import functools

import jax
import jax.numpy as jnp
from jax import lax
from jax.experimental import pallas as pl
from jax.experimental.pallas import tpu as pltpu

F32 = jnp.float32
BF16 = jnp.bfloat16

LRU_HEADS = 8
LRU_CONV = 4
LRU_C = 8.0
MLA_HEADS = 8
QK_NOPE = 128
QK_ROPE = 64
ROPE_THETA = 10000.0
ATTN_SCALE = (QK_NOPE + QK_ROPE) ** -0.5
PAGE_SIZE = 128
CONV_K = 31
RMS_EPS = 1e-6
LN_EPS = 1e-5

LANES = 128
SUBLANES = 8
VMEM_LIMIT_BYTES = 56 * 1024 * 1024

ROW_TILE = 512
IN_ROW_TILE = 1024
IN_COL_TILE = 1536
SEQ_TILE = 256
LRU_SAMPLE_SEQS = 128
CONV_SAMPLE_SEQS = 32
Q_TILE = 128
KV_TILE = 512
PAGES_PER_CHUNK = 16
PAGES_PER_GROUP = 4
PAGE_SLOTS = 3


def _params(*sem):
    return pltpu.CompilerParams(dimension_semantics=sem, vmem_limit_bytes=VMEM_LIMIT_BYTES)


def _dot(a, b):
    return jnp.dot(a, b, preferred_element_type=F32)


def _dot_nt(a, b):
    return lax.dot_general(a, b, (((1,), (1,)), ((), ())), preferred_element_type=F32)


def _rms(x, g):
    return x * lax.rsqrt(jnp.mean(x * x, axis=-1, keepdims=True) + RMS_EPS) * g


def _silu(x):
    return x * jax.nn.sigmoid(x)


def _seq_block(batch, preferred):
    if batch <= preferred:
        return batch
    assert batch % preferred == 0
    return preferred


def _in_proj_kernel(x_ref, g_ref, w_ref, wkr_ref, z_ref, zkr_ref, hn_ref):
    @pl.when(pl.program_id(1) == 0)
    def _():
        hn = _rms(x_ref[...], g_ref[...]).astype(BF16)
        hn_ref[...] = hn
        zkr_ref[...] = _dot(hn, wkr_ref[...])

    z_ref[...] = _dot(hn_ref[...], w_ref[...])


def _in_proj(x, g, w_main, w_kr):
    n, d = x.shape
    cols = w_main.shape[1]
    tm = min(IN_ROW_TILE, n)
    tn = IN_COL_TILE
    return pl.pallas_call(
        _in_proj_kernel,
        grid=(n // tm, cols // tn),
        in_specs=[
            pl.BlockSpec((tm, d), lambda i, j: (i, 0)),
            pl.BlockSpec((1, d), lambda i, j: (0, 0)),
            pl.BlockSpec((d, tn), lambda i, j: (0, j)),
            pl.BlockSpec((d, LANES), lambda i, j: (0, 0)),
        ],
        out_specs=[
            pl.BlockSpec((tm, tn), lambda i, j: (i, j)),
            pl.BlockSpec((tm, LANES), lambda i, j: (i, 0)),
        ],
        out_shape=[jax.ShapeDtypeStruct((n, cols), F32), jax.ShapeDtypeStruct((n, LANES), F32)],
        scratch_shapes=[pltpu.VMEM((tm, d), BF16)],
        compiler_params=_params("parallel", "arbitrary"),
        name="in_proj",
    )(x, g, w_main, w_kr)


def _lru_gates(xc, wa_ref, ba_ref, wx_ref, bx_ref, lam_ref):
    xcb = xc.astype(BF16)
    r = jax.nn.sigmoid(_dot(xcb, wa_ref[...]) + ba_ref[...])
    i = jax.nn.sigmoid(_dot(xcb, wx_ref[...]) + bx_ref[...])
    softplus_neg_lam = jnp.log(1.0 + jnp.exp(-lam_ref[...]))
    a = jnp.exp(-LRU_C * r * softplus_neg_lam)
    return a, jnp.sqrt(1.0 - a * a) * i * xc


def _lru_kernel(zx_ref, zg_ref, buf_ref, h0_ref, cw_ref, cb_ref, wa_ref, ba_ref, wx_ref, bx_ref, lam_ref,
                out_ref, hlast_ref, nbuf_ref, xp_ref, h_ref, a_ref, u_ref, hs_ref, *, tl):
    pad = SUBLANES - (LRU_CONV - 1)

    @pl.when(pl.program_id(1) == 0)
    def _():
        xp_ref[pad:SUBLANES, :] = buf_ref[...]
        h_ref[...] = h0_ref[...]

    x = zx_ref[...]
    xp_ref[SUBLANES:SUBLANES + tl, :] = x
    xc = cb_ref[...] + cw_ref[LRU_CONV - 1:LRU_CONV, :] * x
    for k in range(LRU_CONV - 1):
        xc = xc + cw_ref[k:k + 1, :] * xp_ref[pad + k:pad + k + tl, :]
    tail = xp_ref[pad + tl:SUBLANES + tl, :]
    xp_ref[pad:SUBLANES, :] = tail
    nbuf_ref[...] = tail

    a, u = _lru_gates(xc, wa_ref, ba_ref, wx_ref, bx_ref, lam_ref)
    a_ref[...] = a
    u_ref[...] = u

    def step(t, h):
        h = a_ref[pl.ds(t, 1), :] * h + u_ref[pl.ds(t, 1), :]
        hs_ref[pl.ds(t, 1), :] = h
        return h

    h = lax.fori_loop(0, tl, step, h_ref[...], unroll=SUBLANES)
    h_ref[...] = h
    hlast_ref[...] = h
    out_ref[...] = hs_ref[...] * _silu(zg_ref[...])


def _lru_prompt(z, buf, h0, p, batch, seq):
    w = p['lru_cw'].shape[1]
    tl = min(SEQ_TILE, seq)
    nc = seq // tl
    row = lambda b, c: (b * nc + c, 0)
    vec = pl.BlockSpec((1, w), lambda b, c: (0, 0))
    mat = pl.BlockSpec((w, w), lambda b, c: (0, 0))
    out_a, h_last, nbuf = pl.pallas_call(
        functools.partial(_lru_kernel, tl=tl),
        grid=(batch, nc),
        in_specs=[
            pl.BlockSpec((tl, w), row),
            pl.BlockSpec((tl, w), lambda b, c: (b * nc + c, 1)),
            pl.BlockSpec((None, LRU_CONV - 1, w), lambda b, c: (b, 0, 0)),
            pl.BlockSpec((None, 1, w), lambda b, c: (b, 0, 0)),
            pl.BlockSpec((LRU_CONV, w), lambda b, c: (0, 0)),
            vec, mat, vec, mat, vec, vec,
        ],
        out_specs=[
            pl.BlockSpec((tl, w), row),
            pl.BlockSpec((None, 1, w), lambda b, c: (b, 0, 0)),
            pl.BlockSpec((None, LRU_CONV - 1, w), lambda b, c: (b, 0, 0)),
        ],
        out_shape=[
            jax.ShapeDtypeStruct((batch * seq, w), F32),
            jax.ShapeDtypeStruct((batch, 1, w), F32),
            jax.ShapeDtypeStruct((batch, LRU_CONV - 1, w), F32),
        ],
        scratch_shapes=[
            pltpu.VMEM((tl + SUBLANES, w), F32),
            pltpu.VMEM((1, w), F32),
            pltpu.VMEM((tl, w), F32),
            pltpu.VMEM((tl, w), F32),
            pltpu.VMEM((tl, w), F32),
        ],
        compiler_params=_params("parallel", "arbitrary"),
        name="lru_prompt",
    )(z, z, buf, h0, p['lru_cw'], p['lru_cb'], p['lru_wa'], p['lru_ba'], p['lru_wx'], p['lru_bx'], p['lru_lam'])
    return out_a, h_last[:, 0], nbuf


def _token_slab(refs, k, bb, t):
    return jnp.concatenate([ref[pl.ds(k, bb, stride=t), :] for ref in refs], axis=-1)


def _store_token_slab(stage_ref, k, bb, t, val):
    for c in range(stage_ref.shape[0]):
        stage_ref[c, pl.ds(k, bb, stride=t), :] = val[:, c * LANES:(c + 1) * LANES]


def _unstage(stage_ref, out_ref):
    for c in range(stage_ref.shape[0]):
        out_ref[:, c * LANES:(c + 1) * LANES] = stage_ref[c]


def _lane_tile_specs(rows, width, col0):
    nl = width // LANES
    return [pl.BlockSpec((rows, LANES), lambda i, c=c: (i, col0 * nl + c)) for c in range(nl)]


def _lru_sample_kernel(*refs, t, bb, nl):
    zx_refs, zg_refs = refs[:nl], refs[nl:2 * nl]
    (buf_ref, h0_ref, cw_ref, cb_ref, wa_ref, ba_ref, wx_ref, bx_ref, lam_ref,
     out_ref, hlast_ref, nbuf_ref, stage_ref) = refs[2 * nl:]
    xp = [buf_ref[k] for k in range(LRU_CONV - 1)] + [_token_slab(zx_refs, k, bb, t) for k in range(t)]
    h = h0_ref[...]
    for k in range(t):
        xc = cb_ref[...] + cw_ref[0:1, :] * xp[k]
        for j in range(1, LRU_CONV):
            xc = xc + cw_ref[j:j + 1, :] * xp[k + j]
        a, u = _lru_gates(xc, wa_ref, ba_ref, wx_ref, bx_ref, lam_ref)
        h = a * h + u
        _store_token_slab(stage_ref, k, bb, t, h * _silu(_token_slab(zg_refs, k, bb, t)))
    _unstage(stage_ref, out_ref)
    hlast_ref[...] = h
    for j in range(LRU_CONV - 1):
        nbuf_ref[j] = xp[t + j]


def _lru_sample(z, layer, buf_t, h0, p, batch, t):
    w = p['lru_cw'].shape[1]
    nl = w // LANES
    bb = _seq_block(batch, LRU_SAMPLE_SEQS)
    vec = pl.BlockSpec((1, w), lambda i: (0, 0))
    mat = pl.BlockSpec((w, w), lambda i: (0, 0))
    out_a, h_last, nbuf = pl.pallas_call(
        functools.partial(_lru_sample_kernel, t=t, bb=bb, nl=nl),
        grid=(batch // bb,),
        in_specs=[
            *_lane_tile_specs(bb * t, w, 0),
            *_lane_tile_specs(bb * t, w, 1),
            pl.BlockSpec((None, LRU_CONV - 1, bb, w), lambda i: (layer, 0, i, 0)),
            pl.BlockSpec((None, bb, w), lambda i: (layer, i, 0)),
            pl.BlockSpec((LRU_CONV, w), lambda i: (0, 0)),
            vec, mat, vec, mat, vec, vec,
        ],
        out_specs=[
            pl.BlockSpec((bb * t, w), lambda i: (i, 0)),
            pl.BlockSpec((bb, w), lambda i: (i, 0)),
            pl.BlockSpec((LRU_CONV - 1, bb, w), lambda i: (0, i, 0)),
        ],
        out_shape=[
            jax.ShapeDtypeStruct((batch * t, w), F32),
            jax.ShapeDtypeStruct((batch, w), F32),
            jax.ShapeDtypeStruct((LRU_CONV - 1, batch, w), F32),
        ],
        scratch_shapes=[pltpu.VMEM((nl, bb * t, LANES), F32)],
        compiler_params=_params("parallel"),
        name="lru_sample",
    )(*([z] * (2 * nl)), buf_t, h0, p['lru_cw'], p['lru_cb'], p['lru_wa'], p['lru_ba'], p['lru_wx'], p['lru_bx'],
      p['lru_lam'])
    return out_a, h_last, jnp.transpose(nbuf, (1, 0, 2))


CONV_HIST = CONV_K - 1
CONV_PAD = 32


def _ln_silu_gate(y, lng_ref, lnb_ref, gate):
    mu = jnp.mean(y, axis=-1, keepdims=True)
    yc = y - mu
    yn = yc * lax.rsqrt(jnp.mean(yc * yc, axis=-1, keepdims=True) + LN_EPS) * lng_ref[...] + lnb_ref[...]
    return _silu(yn) * _silu(gate)


def _conv_kernel(zv_ref, zglu_ref, zg_ref, buf_ref, w_ref, b_ref, lng_ref, lnb_ref,
                 out_ref, nbuf_ref, s_ref, win_ref, *, tl):
    base = CONV_PAD - CONV_HIST

    @pl.when(pl.program_id(1) == 0)
    def _():
        s_ref[base:CONV_PAD, :] = buf_ref[...]

    s_ref[CONV_PAD:CONV_PAD + tl, :] = zv_ref[...] * jax.nn.sigmoid(zglu_ref[...])
    y = b_ref[...]
    for shift in range(SUBLANES):
        taps = range(shift, CONV_K, SUBLANES)
        rows = taps[-1] - shift + tl
        win_ref[shift, 0:rows, :] = s_ref[base + shift:base + shift + rows, :]
        for k in taps:
            y = y + w_ref[k:k + 1, :] * win_ref[shift, k - shift:k - shift + tl, :]
    tail = s_ref[base + tl:CONV_PAD + tl, :]
    s_ref[base:CONV_PAD, :] = tail
    nbuf_ref[...] = tail
    out_ref[...] = _ln_silu_gate(y, lng_ref, lnb_ref, zg_ref[...])


def _conv_prompt(z, buf, p, batch, seq):
    cw = p['dw_w'].shape[1]
    col0 = p['conv_col0']
    tl = min(SEQ_TILE, seq)
    nc = seq // tl
    vec = pl.BlockSpec((1, cw), lambda b_, c: (0, 0))
    zspec = lambda off: pl.BlockSpec((tl, cw), lambda b_, c: (b_ * nc + c, col0 + off))
    return pl.pallas_call(
        functools.partial(_conv_kernel, tl=tl),
        grid=(batch, nc),
        in_specs=[
            zspec(0), zspec(1), zspec(2),
            pl.BlockSpec((None, CONV_HIST, cw), lambda b_, c: (b_, 0, 0)),
            pl.BlockSpec((CONV_K, cw), lambda b_, c: (0, 0)),
            vec, vec, vec,
        ],
        out_specs=[
            pl.BlockSpec((tl, cw), lambda b_, c: (b_ * nc + c, 0)),
            pl.BlockSpec((None, CONV_HIST, cw), lambda b_, c: (b_, 0, 0)),
        ],
        out_shape=[
            jax.ShapeDtypeStruct((batch * seq, cw), F32),
            jax.ShapeDtypeStruct((batch, CONV_HIST, cw), F32),
        ],
        scratch_shapes=[pltpu.VMEM((tl + CONV_PAD, cw), F32),
                        pltpu.VMEM((SUBLANES, tl + CONV_PAD - SUBLANES, cw), F32)],
        compiler_params=_params("parallel", "arbitrary"),
        name="conv_prompt",
    )(z, z, z, buf, p['dw_w'], p['dw_b'], p['ln_g'], p['ln_b'])


def _conv_sample_kernel(*refs, t, bb, nl):
    zv_refs, zglu_refs, zg_refs = refs[:nl], refs[nl:2 * nl], refs[2 * nl:3 * nl]
    buf_ref, w_ref, b_ref, lng_ref, lnb_ref, out_ref, nbuf_ref, u_ref, stage_ref = refs[3 * nl:]
    for k in range(t):
        u_ref[k] = _token_slab(zv_refs, k, bb, t) * jax.nn.sigmoid(_token_slab(zglu_refs, k, bb, t))

    def tap(j):
        return buf_ref[j] if j < CONV_HIST else u_ref[j - CONV_HIST]

    for k in range(t):
        y = b_ref[...] + w_ref[0:1, :] * tap(k)
        for j in range(1, CONV_K):
            y = y + w_ref[j:j + 1, :] * tap(k + j)
        _store_token_slab(stage_ref, k, bb, t, _ln_silu_gate(y, lng_ref, lnb_ref, _token_slab(zg_refs, k, bb, t)))
    _unstage(stage_ref, out_ref)
    for j in range(CONV_HIST):
        nbuf_ref[j] = tap(t + j)


def _conv_sample(z, layer, buf_t, p, batch, t):
    cw = p['dw_w'].shape[1]
    nl = cw // LANES
    col0 = p['conv_col0']
    bb = _seq_block(batch, CONV_SAMPLE_SEQS)
    vec = pl.BlockSpec((1, cw), lambda i: (0, 0))
    out_c, nbuf = pl.pallas_call(
        functools.partial(_conv_sample_kernel, t=t, bb=bb, nl=nl),
        grid=(batch // bb,),
        in_specs=[
            *_lane_tile_specs(bb * t, cw, col0),
            *_lane_tile_specs(bb * t, cw, col0 + 1),
            *_lane_tile_specs(bb * t, cw, col0 + 2),
            pl.BlockSpec((None, CONV_HIST, bb, cw), lambda i: (layer, 0, i, 0)),
            pl.BlockSpec((CONV_K, cw), lambda i: (0, 0)),
            vec, vec, vec,
        ],
        out_specs=[
            pl.BlockSpec((bb * t, cw), lambda i: (i, 0)),
            pl.BlockSpec((CONV_HIST, bb, cw), lambda i: (0, i, 0)),
        ],
        out_shape=[
            jax.ShapeDtypeStruct((batch * t, cw), F32),
            jax.ShapeDtypeStruct((CONV_HIST, batch, cw), F32),
        ],
        scratch_shapes=[pltpu.VMEM((t, bb, cw), F32), pltpu.VMEM((nl, bb * t, LANES), F32)],
        compiler_params=_params("parallel"),
        name="conv_sample",
    )(*([z] * (3 * nl)), buf_t, p['dw_w'], p['dw_b'], p['ln_g'], p['ln_b'])
    return out_c, jnp.transpose(nbuf, (1, 0, 2))


def _rope128(v, table):
    a = v * table
    return a + pltpu.roll(a, QK_ROPE, 1)


def _mla_proj_kernel(zq_ref, zkv_ref, zkr_ref, qn_ref, kvn_ref, wn_ref, wr_ref, wuk_ref, tab_ref,
                     qa_ref, qr_ref, ckv_ref, ckvb_ref, krope_ref, krp_ref):
    tab = tab_ref[...]
    cq = _rms(zq_ref[...], qn_ref[...]).astype(BF16)
    q_nope = _dot(cq, wn_ref[...])
    q_rope = _dot(cq, wr_ref[...])
    for h in range(MLA_HEADS):
        qh = q_nope[:, h * QK_NOPE:(h + 1) * QK_NOPE].astype(BF16)
        qa_ref[h] = _dot(qh, wuk_ref[h]).astype(qa_ref.dtype)
        qr_ref[h] = _rope128(q_rope[:, h * LANES:(h + 1) * LANES], tab).astype(qr_ref.dtype)
    ckv = _rms(zkv_ref[...], kvn_ref[...])
    ckv_ref[...] = ckv
    ckvb_ref[...] = ckv.astype(BF16)
    kro = _rope128(zkr_ref[...], tab)
    krope_ref[...] = kro[:, :QK_ROPE]
    lane = lax.broadcasted_iota(jnp.int32, kro.shape, 1)
    krp_ref[...] = jnp.where(lane < QK_ROPE, kro, 0.0).astype(BF16)


def _mla_proj(z, zkr, p, table, q_dtype):
    n = z.shape[0]
    r = p['qn'].shape[1]
    tm = min(ROW_TILE, n)
    nt = table.shape[0] // tm
    wn, wr, wuk = p['wn'], p['wr'], p['wuk']
    full = lambda shape: pl.BlockSpec(shape, lambda i: (0,) * len(shape))
    return pl.pallas_call(
        _mla_proj_kernel,
        grid=(n // tm,),
        in_specs=[
            pl.BlockSpec((tm, r), lambda i: (i, 2)),
            pl.BlockSpec((tm, r), lambda i: (i, 3)),
            pl.BlockSpec((tm, LANES), lambda i: (i, 0)),
            full((1, r)), full((1, r)),
            full(wn.shape), full(wr.shape), full(wuk.shape),
            pl.BlockSpec((tm, LANES), lambda i: (i % nt, 0)),
        ],
        out_specs=[
            pl.BlockSpec((MLA_HEADS, tm, r), lambda i: (0, i, 0)),
            pl.BlockSpec((MLA_HEADS, tm, LANES), lambda i: (0, i, 0)),
            pl.BlockSpec((tm, r), lambda i: (i, 0)),
            pl.BlockSpec((tm, r), lambda i: (i, 0)),
            pl.BlockSpec((tm, QK_ROPE), lambda i: (i, 0)),
            pl.BlockSpec((tm, LANES), lambda i: (i, 0)),
        ],
        out_shape=[
            jax.ShapeDtypeStruct((MLA_HEADS, n, r), q_dtype),
            jax.ShapeDtypeStruct((MLA_HEADS, n, LANES), q_dtype),
            jax.ShapeDtypeStruct((n, r), F32),
            jax.ShapeDtypeStruct((n, r), BF16),
            jax.ShapeDtypeStruct((n, QK_ROPE), F32),
            jax.ShapeDtypeStruct((n, LANES), BF16),
        ],
        compiler_params=_params("parallel"),
        name="mla_proj",
    )(z, z, zkr, p['qn'], p['kvn'], wn, wr, wuk, table)


def _local_softmax(s, v):
    m = jnp.max(s, axis=-1, keepdims=True)
    p = jnp.exp(s - m)
    return m, jnp.sum(p, axis=-1, keepdims=True), _dot(p.astype(BF16), v)


def _merge_softmax(parts):
    m = parts[0][0]
    for part in parts[1:]:
        m = jnp.maximum(m, part[0])
    l = acc = None
    for m_g, l_g, acc_g in parts:
        w = jnp.exp(m_g - m)
        l = l_g * w if l is None else l + l_g * w
        acc = acc_g * w if acc is None else acc + acc_g * w
    return m, l, acc


def _prompt_attn_kernel(qi_ref, kj_ref, qa_ref, qr_ref, kc_ref, kr_ref, v_ref, wuv_ref, zg_ref,
                        out_ref, s_ref, m_ref, l_ref, acc_ref, *, tq, tk, n_tiles):
    n = pl.program_id(1)
    rows = MLA_HEADS * tq

    def scores():
        qa = qa_ref[...].reshape(rows, qa_ref.shape[-1])
        qr = qr_ref[...].reshape(rows, LANES)
        return (_dot_nt(qa, kc_ref[...]) + _dot_nt(qr, kr_ref[...])) * ATTN_SCALE

    prev = jnp.maximum(n - 1, 0)
    i = qi_ref[prev]
    j = kj_ref[prev]

    def absorb():
        q_pos = i * tq + (lax.broadcasted_iota(jnp.int32, (rows, 1), 0) & (tq - 1))
        k_pos = j * tk + lax.broadcasted_iota(jnp.int32, (1, tk), 1)
        s = jnp.where(k_pos <= q_pos, s_ref[...], -jnp.inf)
        restart = j == 0
        m_old = jnp.where(restart, -jnp.inf, m_ref[...])
        m_new = jnp.maximum(m_old, jnp.max(s, axis=-1, keepdims=True))
        corr = jnp.exp(m_old - m_new)
        p = jnp.exp(s - m_new)
        l_ref[...] = jnp.where(restart, 0.0, l_ref[...]) * corr + jnp.sum(p, axis=-1, keepdims=True)
        acc_ref[...] = jnp.where(restart, 0.0, acc_ref[...]) * corr + _dot(p.astype(BF16), v_ref[...])
        m_ref[...] = m_new

    @pl.when(n == 0)
    def _():
        m_ref[...] = jnp.full(m_ref.shape, -jnp.inf, F32)
        l_ref[...] = jnp.zeros(l_ref.shape, F32)
        acc_ref[...] = jnp.zeros(acc_ref.shape, F32)
        s_ref[...] = scores()

    @pl.when(jnp.logical_and(n > 0, n < n_tiles))
    def _():
        s_next = scores()
        absorb()
        s_ref[...] = s_next

    @pl.when(n == n_tiles)
    def _():
        absorb()

    @pl.when(jnp.logical_and(n > 0, j == (i * tq + tq - 1) // tk))
    def _():
        ob = (acc_ref[...] / l_ref[...]).astype(BF16)
        heads = [_dot(ob[h * tq:(h + 1) * tq], wuv_ref[h]) for h in range(MLA_HEADS)]
        out_ref[...] = jnp.concatenate(heads, axis=-1) * _silu(zg_ref[...])


def _causal_tiles(seq, tq, tk):
    tiles = [(i, j) for i in range(seq // tq) for j in range((i * tq + tq - 1) // tk + 1)]
    tiles.append(tiles[-1])
    return jnp.array([t[0] for t in tiles], jnp.int32), jnp.array([t[1] for t in tiles], jnp.int32), len(tiles) - 1


def _prompt_attention(qa, qr, kc, kr, wuv, z, batch, seq):
    r = qa.shape[-1]
    tq = min(Q_TILE, seq)
    tk = min(KV_TILE, seq)
    assert tq & (tq - 1) == 0
    nq, nk = seq // tq, seq // tk
    width = wuv.shape[0] * wuv.shape[2]
    qi, kj, n_tiles = _causal_tiles(seq, tq, tk)
    lag = lambda n: jnp.maximum(n - 1, 0)
    grid_spec = pltpu.PrefetchScalarGridSpec(
        num_scalar_prefetch=2,
        grid=(batch, n_tiles + 1),
        in_specs=[
            pl.BlockSpec((MLA_HEADS, tq, r), lambda b, n, qi, kj: (0, b * nq + qi[n], 0)),
            pl.BlockSpec((MLA_HEADS, tq, LANES), lambda b, n, qi, kj: (0, b * nq + qi[n], 0)),
            pl.BlockSpec((tk, r), lambda b, n, qi, kj: (b * nk + kj[n], 0)),
            pl.BlockSpec((tk, LANES), lambda b, n, qi, kj: (b * nk + kj[n], 0)),
            pl.BlockSpec((tk, r), lambda b, n, qi, kj: (b * nk + kj[lag(n)], 0)),
            pl.BlockSpec(wuv.shape, lambda b, n, qi, kj: (0, 0, 0)),
            pl.BlockSpec((tq, width), lambda b, n, qi, kj: (b * nq + qi[lag(n)], 2)),
        ],
        out_specs=pl.BlockSpec((tq, width), lambda b, n, qi, kj: (b * nq + qi[lag(n)], 0)),
        scratch_shapes=[
            pltpu.VMEM((MLA_HEADS * tq, tk), F32),
            pltpu.VMEM((MLA_HEADS * tq, 1), F32),
            pltpu.VMEM((MLA_HEADS * tq, 1), F32),
            pltpu.VMEM((MLA_HEADS * tq, r), F32),
        ],
    )
    return pl.pallas_call(
        functools.partial(_prompt_attn_kernel, tq=tq, tk=tk, n_tiles=n_tiles),
        grid_spec=grid_spec,
        out_shape=jax.ShapeDtypeStruct((batch * seq, width), F32),
        compiler_params=_params("parallel", "arbitrary"),
        name="prompt_attention",
    )(qi, kj, qa, qr, kc, kr, kc, wuv, z)


def _sample_attn_kernel(pt_ref, qa_ref, qr_ref, kcn_ref, krn_ref, wuv_ref, zg_ref, lat_hbm, rope_hbm,
                        out_ref, lat_buf, rope_buf, sem, kb_buf, m_ref, l_ref, acc_ref,
                        *, layer, t, n_pages, chunk, group):
    b = pl.program_id(0)
    n_chunks = n_pages // chunk
    total = pl.num_programs(0) * n_chunks
    first = b * n_chunks
    rows = MLA_HEADS * t
    r = qa_ref.shape[-1]

    def page_copies(n):
        slot = n % PAGE_SLOTS
        copies = []
        for g in range(chunk):
            page = pt_ref[n * chunk + g]
            copies.append(pltpu.make_async_copy(lat_hbm.at[layer, page], lat_buf.at[slot, g], sem.at[0, slot]))
            copies.append(pltpu.make_async_copy(rope_hbm.at[layer, page], rope_buf.at[slot, g], sem.at[1, slot]))
        return copies

    def fetch(n):
        @pl.when(n < total)
        def _():
            for cp in page_copies(n):
                cp.start()

    def consume(n):
        fetch(n + PAGE_SLOTS - 1)
        for cp in page_copies(n):
            cp.wait()

    @pl.when(b == 0)
    def _():
        for n in range(PAGE_SLOTS - 1):
            fetch(n)

    qa = qa_ref[...].reshape(rows, r).astype(BF16)
    qr = qr_ref[...].reshape(rows, LANES)[:, :QK_ROPE].astype(BF16)

    def scores(n, kb_slot):
        slot = n % PAGE_SLOTS
        kb = lat_buf[slot].reshape(chunk * PAGE_SIZE, r).astype(BF16)
        kb_buf[kb_slot] = kb
        s_rope = jnp.concatenate([_dot(qr, rope_buf[slot, g].astype(BF16)) for g in range(chunk)], axis=-1)
        return (_dot_nt(qa, kb) + s_rope) * ATTN_SCALE

    def absorb(s, kb_slot):
        parts = [(m_ref[...], l_ref[...], acc_ref[...])]
        for k0 in range(0, chunk * PAGE_SIZE, group * PAGE_SIZE):
            k1 = k0 + group * PAGE_SIZE
            parts.append(_local_softmax(s[:, k0:k1], kb_buf[kb_slot, k0:k1, :]))
        m, l, acc = _merge_softmax(parts)
        m_ref[...] = m
        l_ref[...] = l
        acc_ref[...] = acc

    consume(first)
    s = scores(first, 0)

    kc = jnp.concatenate([kcn_ref[...], jnp.zeros((LANES - t, r), F32)], axis=0).astype(BF16)
    kr = jnp.concatenate([krn_ref[...], jnp.zeros((LANES - t, QK_ROPE), F32)], axis=0).astype(BF16)
    s_tok = (_dot_nt(qa, kc) + _dot_nt(qr, kr)) * ATTN_SCALE
    q_tok = lax.broadcasted_iota(jnp.int32, (rows, 1), 0) % t
    k_tok = lax.broadcasted_iota(jnp.int32, (1, LANES), 1)
    m0, l0, acc0 = _local_softmax(jnp.where(k_tok <= q_tok, s_tok, -jnp.inf), kc)
    m_ref[...] = m0
    l_ref[...] = l0
    acc_ref[...] = acc0

    for c in range(1, n_chunks):
        consume(first + c)
        s_next = scores(first + c, c % 2)
        absorb(s, (c - 1) % 2)
        s = s_next
    absorb(s, (n_chunks - 1) % 2)

    ob = (acc_ref[...] / l_ref[...]).astype(BF16)
    heads = [_dot(ob, wuv_ref[h])[h * t:(h + 1) * t] for h in range(MLA_HEADS)]
    out_ref[...] = jnp.concatenate(heads, axis=-1) * _silu(zg_ref[...])


def _sample_attention(qa, qr, ckv, krope, cache_lat, cache_rope_t, layer, page_table, wuv, z, batch, t):
    assert t == SUBLANES
    r = qa.shape[-1]
    n_pages = page_table.shape[1]
    chunk = min(PAGES_PER_CHUNK, n_pages)
    group = min(PAGES_PER_GROUP, chunk)
    assert n_pages % chunk == 0 and chunk % group == 0
    width = wuv.shape[0] * wuv.shape[2]
    grid_spec = pltpu.PrefetchScalarGridSpec(
        num_scalar_prefetch=1,
        grid=(batch,),
        in_specs=[
            pl.BlockSpec((MLA_HEADS, t, r), lambda b, pt_ref: (0, b, 0)),
            pl.BlockSpec((MLA_HEADS, t, LANES), lambda b, pt_ref: (0, b, 0)),
            pl.BlockSpec((t, r), lambda b, pt_ref: (b, 0)),
            pl.BlockSpec((t, QK_ROPE), lambda b, pt_ref: (b, 0)),
            pl.BlockSpec(wuv.shape, lambda b, pt_ref: (0, 0, 0)),
            pl.BlockSpec((t, width), lambda b, pt_ref: (b, 2)),
            pl.BlockSpec(memory_space=pl.ANY),
            pl.BlockSpec(memory_space=pl.ANY),
        ],
        out_specs=pl.BlockSpec((t, width), lambda b, pt_ref: (b, 0)),
        scratch_shapes=[
            pltpu.VMEM((PAGE_SLOTS, chunk, PAGE_SIZE, r), F32),
            pltpu.VMEM((PAGE_SLOTS, chunk, QK_ROPE, PAGE_SIZE), F32),
            pltpu.SemaphoreType.DMA((2, PAGE_SLOTS)),
            pltpu.VMEM((2, chunk * PAGE_SIZE, r), BF16),
            pltpu.VMEM((MLA_HEADS * t, 1), F32),
            pltpu.VMEM((MLA_HEADS * t, 1), F32),
            pltpu.VMEM((MLA_HEADS * t, r), F32),
        ],
    )
    return pl.pallas_call(
        functools.partial(_sample_attn_kernel, layer=layer, t=t, n_pages=n_pages, chunk=chunk, group=group),
        grid_spec=grid_spec,
        out_shape=jax.ShapeDtypeStruct((batch * t, width), F32),
        compiler_params=_params("arbitrary"),
        name="sample_attention",
    )(page_table.reshape(-1), qa, qr, ckv, krope, wuv, z, cache_lat, cache_rope_t)


def _out_proj_kernel(a_ref, b_ref, c_ref, x_ref, wa_ref, wb_ref, wc_ref, g_ref, y_ref, *, final_norm):
    y = x_ref[...] + _dot(a_ref[...].astype(BF16), wa_ref[...])
    y = y + _dot(b_ref[...].astype(BF16), wb_ref[...])
    y = y + _dot(c_ref[...].astype(BF16), wc_ref[...])
    if final_norm:
        y = _rms(y, g_ref[...])
    y_ref[...] = y


def _out_proj(out_a, out_b, out_c, x, wa, wb, wc, g, final_norm):
    n, d = x.shape
    tm = min(ROW_TILE // 2, n)
    rows = lambda width: pl.BlockSpec((tm, width), lambda i: (i, 0))
    full = lambda arr: pl.BlockSpec(arr.shape, lambda i: (0, 0))
    return pl.pallas_call(
        functools.partial(_out_proj_kernel, final_norm=final_norm),
        grid=(n // tm,),
        in_specs=[rows(out_a.shape[1]), rows(out_b.shape[1]), rows(out_c.shape[1]), rows(d),
                  full(wa), full(wb), full(wc), full(g)],
        out_specs=rows(d),
        out_shape=jax.ShapeDtypeStruct((n, d), F32),
        compiler_params=_params("parallel"),
        name="out_proj",
    )(out_a, out_b, out_c, x, wa, wb, wc, g)


def _rope_table(pos, rows):
    half = QK_ROPE // 2
    freqs = ROPE_THETA ** (-jnp.arange(half, dtype=F32) / half)
    ang = pos.astype(F32)[:, None] * freqs[None, :]
    cos, sin = jnp.cos(ang), jnp.sin(ang)
    table = jnp.concatenate([cos, cos, -sin, sin], axis=-1)
    reps = max(1, rows // table.shape[0])
    return jnp.tile(table, (reps, 1))


def _block_diag(w):
    h, d, _ = w.shape
    eye = jnp.eye(h, dtype=w.dtype)
    return (eye[:, None, :, None] * w[:, :, None, :]).reshape(h * d, h * d)


def _swap_halves(w):
    half = w.shape[-1] // 2
    return jnp.concatenate([w[..., half:], w[..., :half]], axis=-1)


def _layer_weights(l, norm_g, w_in, w_out, lru_conv_w, lru_conv_b, lru_w_a, lru_b_a, lru_w_x, lru_b_x, lru_lambda,
                   mla_q_norm, mla_kv_norm, mla_w_uq, mla_w_uk, mla_w_uv, conv_dw_w, conv_dw_b, conv_ln_g,
                   conv_ln_b):
    lw = lru_conv_w.shape[2]
    q_rank = mla_q_norm.shape[1]
    kv_rank = mla_kv_norm.shape[1]
    mla_w = mla_w_uv.shape[2] * mla_w_uv.shape[3]
    cw = conv_dw_w.shape[2]
    sizes = (lw, lw, q_rank, kv_rank, QK_ROPE, mla_w, cw, cw, cw)
    offs = [0]
    for s in sizes:
        offs.append(offs[-1] + s)
    w = w_in[l]
    col = lambda k: w[:, offs[k]:offs[k + 1]]
    w_main = jnp.concatenate([col(0), col(1), col(2), col(3), col(5), col(6), col(7), col(8)], axis=1).astype(BF16)
    w_kr = jnp.concatenate([col(4), _swap_halves(col(4))], axis=1).astype(BF16)
    wuq = mla_w_uq[l]
    wn = wuq[:, :, :QK_NOPE].reshape(q_rank, -1).astype(BF16)
    wrope = wuq[:, :, QK_NOPE:]
    wr = jnp.concatenate([wrope, _swap_halves(wrope)], axis=-1).reshape(q_rank, -1).astype(BF16)
    wuk = jnp.transpose(mla_w_uk[l], (1, 2, 0)).astype(BF16)
    wuv = jnp.transpose(mla_w_uv[l], (1, 0, 2)).astype(BF16)
    wo = w_out[l].astype(BF16)
    row = lambda v: v[l][None, :]
    return dict(
        norm_g=row(norm_g), w_main=w_main, w_kr=w_kr,
        lru_cw=lru_conv_w[l], lru_cb=row(lru_conv_b),
        lru_wa=_block_diag(lru_w_a[l]).astype(BF16), lru_ba=row(lru_b_a),
        lru_wx=_block_diag(lru_w_x[l]).astype(BF16), lru_bx=row(lru_b_x), lru_lam=row(lru_lambda),
        qn=row(mla_q_norm), kvn=row(mla_kv_norm), wn=wn, wr=wr, wuk=wuk, wuv=wuv,
        dw_w=conv_dw_w[l], dw_b=row(conv_dw_b), ln_g=row(conv_ln_g), ln_b=row(conv_ln_b),
        wo_a=wo[:lw], wo_b=wo[lw:lw + mla_w], wo_c=wo[lw + mla_w:],
        conv_col0=(2 * lw + q_rank + kv_rank + mla_w) // cw,
    )


def _mixer_layer(x, p, table, lru_fn, conv_fn, attend, q_dtype, final_g):
    z, zkr = _in_proj(x, p['norm_g'], p['w_main'], p['w_kr'])
    out_a, h_last, lru_buf_new = lru_fn(z, p)
    out_c, conv_buf_new = conv_fn(z, p)
    qa, qr, ckv, ckv_bf, krope, krp = _mla_proj(z, zkr, p, table, q_dtype)
    out_b = attend(qa, qr, ckv, ckv_bf, krope, krp, p['wuv'], z)
    y = _out_proj(out_a, out_b, out_c, x, p['wo_a'], p['wo_b'], p['wo_c'],
                  final_g if final_g is not None else p['norm_g'], final_g is not None)
    return y, (ckv, krope, h_last, lru_buf_new, conv_buf_new)


def kernel(x_prompt, x_sample, cache_kv_latent, cache_k_rope, page_table, state_lru_h, state_lru_conv, state_conv, norm_g, w_in, w_out, lru_conv_w, lru_conv_b, lru_w_a, lru_b_a, lru_w_x, lru_b_x, lru_lambda, mla_q_norm, mla_kv_norm, mla_w_uq, mla_w_uk, mla_w_uv, conv_dw_w, conv_dw_b, conv_ln_g, conv_ln_b, final_norm_g):
    b_p, seq, d_model = x_prompt.shape
    b_s, dec_seq, _ = x_sample.shape
    depth = norm_g.shape[0]
    lw = lru_conv_w.shape[2]
    cw = conv_dw_w.shape[2]
    past_len = page_table.shape[1] * PAGE_SIZE
    n_p, n_s = b_p * seq, b_s * dec_seq
    table_p = _rope_table(jnp.arange(seq, dtype=F32), min(ROW_TILE, n_p))
    table_s = _rope_table(past_len + jnp.arange(dec_seq, dtype=F32), min(ROW_TILE, n_s))
    zero_h = jnp.zeros((b_p, 1, lw), F32)
    zero_lbuf = jnp.zeros((b_p, LRU_CONV - 1, lw), F32)
    zero_cbuf = jnp.zeros((b_p, CONV_K - 1, cw), F32)
    lru_conv_t = jnp.transpose(state_lru_conv, (0, 2, 1, 3))
    conv_t = jnp.transpose(state_conv, (0, 2, 1, 3))
    cache_rope_t = jnp.swapaxes(cache_k_rope, 2, 3)
    final_g = final_norm_g[None, :]

    xp = x_prompt.reshape(n_p, d_model)
    xs = x_sample.reshape(n_s, d_model)
    st_p, st_s = [], []
    for l in range(depth):
        p = _layer_weights(l, norm_g, w_in, w_out, lru_conv_w, lru_conv_b, lru_w_a, lru_b_a, lru_w_x, lru_b_x,
                           lru_lambda, mla_q_norm, mla_kv_norm, mla_w_uq, mla_w_uk, mla_w_uv, conv_dw_w,
                           conv_dw_b, conv_ln_g, conv_ln_b)
        last = final_g if l == depth - 1 else None

        def attend_p(qa, qr, ckv, ckv_bf, krope, krp, wuv, z):
            return _prompt_attention(qa, qr, ckv_bf, krp, wuv, z, b_p, seq)

        def attend_s(qa, qr, ckv, ckv_bf, krope, krp, wuv, z, l=l):
            return _sample_attention(qa, qr, ckv, krope, cache_kv_latent, cache_rope_t, l, page_table, wuv, z,
                                     b_s, dec_seq)

        xp, sp = _mixer_layer(
            xp, p, table_p,
            lambda z, p: _lru_prompt(z, zero_lbuf, zero_h, p, b_p, seq),
            lambda z, p: _conv_prompt(z, zero_cbuf, p, b_p, seq),
            attend_p, BF16, last)
        xs, ss = _mixer_layer(
            xs, p, table_s,
            lambda z, p, l=l: _lru_sample(z, l, lru_conv_t, state_lru_h, p, b_s, dec_seq),
            lambda z, p, l=l: _conv_sample(z, l, conv_t, p, b_s, dec_seq),
            attend_s, F32, last)
        st_p.append(sp)
        st_s.append(ss)

    def stack(states, k, shape):
        return jnp.stack([s[k].reshape(shape) for s in states])

    r = mla_kv_norm.shape[1]
    return (xp.reshape(b_p, seq, d_model), xs.reshape(b_s, dec_seq, d_model),
            stack(st_p, 0, (b_p, seq, r)), stack(st_p, 1, (b_p, seq, QK_ROPE)), stack(st_p, 2, (b_p, lw)),
            stack(st_p, 3, (b_p, LRU_CONV - 1, lw)), stack(st_p, 4, (b_p, CONV_K - 1, cw)),
            stack(st_s, 0, (b_s, dec_seq, r)), stack(st_s, 1, (b_s, dec_seq, QK_ROPE)), stack(st_s, 2, (b_s, lw)),
            stack(st_s, 3, (b_s, LRU_CONV - 1, lw)), stack(st_s, 4, (b_s, CONV_K - 1, cw)))
```

```python
import functools

import jax
import jax.numpy as jnp
from jax import lax
from jax.experimental import pallas as pl
from jax.experimental.pallas import tpu as pltpu

F32 = jnp.float32
BF16 = jnp.bfloat16

LRU_HEADS = 8
LRU_CONV = 4
LRU_C = 8.0
MLA_HEADS = 8
QK_NOPE = 128
QK_ROPE = 64
ROPE_THETA = 10000.0
ATTN_SCALE = (QK_NOPE + QK_ROPE) ** -0.5
PAGE_SIZE = 128
CONV_K = 31
RMS_EPS = 1e-6
LN_EPS = 1e-5

LANES = 128
SUBLANES = 8
VMEM_LIMIT_BYTES = 56 * 1024 * 1024

ROW_TILE = 512
IN_ROW_TILE = 1024
IN_COL_TILE = 1536
SEQ_TILE = 256
LRU_SAMPLE_SEQS = 128
CONV_SAMPLE_SEQS = 32
Q_TILE = 128
KV_TILE = 512
PAGES_PER_CHUNK = 32
SEQS_PER_STEP = 2
PAGES_PER_GROUP = 4
PAGE_SLOTS = 3


def _params(*sem):
    return pltpu.CompilerParams(dimension_semantics=sem, vmem_limit_bytes=VMEM_LIMIT_BYTES)


def _dot(a, b):
    return jnp.dot(a, b, preferred_element_type=F32)


def _dot_nt(a, b):
    return lax.dot_general(a, b, (((1,), (1,)), ((), ())), preferred_element_type=F32)


def _rms(x, g):
    return x * lax.rsqrt(jnp.mean(x * x, axis=-1, keepdims=True) + RMS_EPS) * g


def _silu(x):
    return x * jax.nn.sigmoid(x)


def _seq_block(batch, preferred):
    if batch <= preferred:
        return batch
    assert batch % preferred == 0
    return preferred


def _in_proj_kernel(x_ref, g_ref, w_ref, wkr_ref, z_ref, zkr_ref, hn_ref):
    @pl.when(pl.program_id(1) == 0)
    def _():
        hn = _rms(x_ref[...], g_ref[...]).astype(BF16)
        hn_ref[...] = hn
        zkr_ref[...] = _dot(hn, wkr_ref[...])

    z_ref[...] = _dot(hn_ref[...], w_ref[...])


def _in_proj(x, g, w_main, w_kr):
    n, d = x.shape
    cols = w_main.shape[1]
    tm = min(IN_ROW_TILE, n)
    tn = IN_COL_TILE
    return pl.pallas_call(
        _in_proj_kernel,
        grid=(n // tm, cols // tn),
        in_specs=[
            pl.BlockSpec((tm, d), lambda i, j: (i, 0)),
            pl.BlockSpec((1, d), lambda i, j: (0, 0)),
            pl.BlockSpec((d, tn), lambda i, j: (0, j)),
            pl.BlockSpec((d, LANES), lambda i, j: (0, 0)),
        ],
        out_specs=[
            pl.BlockSpec((tm, tn), lambda i, j: (i, j)),
            pl.BlockSpec((tm, LANES), lambda i, j: (i, 0)),
        ],
        out_shape=[jax.ShapeDtypeStruct((n, cols), F32), jax.ShapeDtypeStruct((n, LANES), F32)],
        scratch_shapes=[pltpu.VMEM((tm, d), BF16)],
        compiler_params=_params("parallel", "arbitrary"),
        name="in_proj",
    )(x, g, w_main, w_kr)


def _lru_gates(xc, wa_ref, ba_ref, wx_ref, bx_ref, lam_ref):
    xcb = xc.astype(BF16)
    r = jax.nn.sigmoid(_dot(xcb, wa_ref[...]) + ba_ref[...])
    i = jax.nn.sigmoid(_dot(xcb, wx_ref[...]) + bx_ref[...])
    softplus_neg_lam = jnp.log(1.0 + jnp.exp(-lam_ref[...]))
    a = jnp.exp(-LRU_C * r * softplus_neg_lam)
    return a, jnp.sqrt(1.0 - a * a) * i * xc


def _lru_kernel(zx_ref, zg_ref, buf_ref, h0_ref, cw_ref, cb_ref, wa_ref, ba_ref, wx_ref, bx_ref, lam_ref,
                out_ref, hlast_ref, nbuf_ref, xp_ref, h_ref, a_ref, u_ref, hs_ref, *, tl):
    pad = SUBLANES - (LRU_CONV - 1)

    @pl.when(pl.program_id(1) == 0)
    def _():
        xp_ref[pad:SUBLANES, :] = buf_ref[...]
        h_ref[...] = h0_ref[...]

    x = zx_ref[...]
    xp_ref[SUBLANES:SUBLANES + tl, :] = x
    xc = cb_ref[...] + cw_ref[LRU_CONV - 1:LRU_CONV, :] * x
    for k in range(LRU_CONV - 1):
        xc = xc + cw_ref[k:k + 1, :] * xp_ref[pad + k:pad + k + tl, :]
    tail = xp_ref[pad + tl:SUBLANES + tl, :]
    xp_ref[pad:SUBLANES, :] = tail
    nbuf_ref[...] = tail

    a, u = _lru_gates(xc, wa_ref, ba_ref, wx_ref, bx_ref, lam_ref)
    a_ref[...] = a
    u_ref[...] = u

    def step(t, h):
        h = a_ref[pl.ds(t, 1), :] * h + u_ref[pl.ds(t, 1), :]
        hs_ref[pl.ds(t, 1), :] = h
        return h

    h = lax.fori_loop(0, tl, step, h_ref[...], unroll=SUBLANES)
    h_ref[...] = h
    hlast_ref[...] = h
    out_ref[...] = hs_ref[...] * _silu(zg_ref[...])


def _lru_prompt(z, buf, h0, p, batch, seq):
    w = p['lru_cw'].shape[1]
    tl = min(SEQ_TILE, seq)
    nc = seq // tl
    row = lambda b, c: (b * nc + c, 0)
    vec = pl.BlockSpec((1, w), lambda b, c: (0, 0))
    mat = pl.BlockSpec((w, w), lambda b, c: (0, 0))
    out_a, h_last, nbuf = pl.pallas_call(
        functools.partial(_lru_kernel, tl=tl),
        grid=(batch, nc),
        in_specs=[
            pl.BlockSpec((tl, w), row),
            pl.BlockSpec((tl, w), lambda b, c: (b * nc + c, 1)),
            pl.BlockSpec((None, LRU_CONV - 1, w), lambda b, c: (b, 0, 0)),
            pl.BlockSpec((None, 1, w), lambda b, c: (b, 0, 0)),
            pl.BlockSpec((LRU_CONV, w), lambda b, c: (0, 0)),
            vec, mat, vec, mat, vec, vec,
        ],
        out_specs=[
            pl.BlockSpec((tl, w), row),
            pl.BlockSpec((None, 1, w), lambda b, c: (b, 0, 0)),
            pl.BlockSpec((None, LRU_CONV - 1, w), lambda b, c: (b, 0, 0)),
        ],
        out_shape=[
            jax.ShapeDtypeStruct((batch * seq, w), F32),
            jax.ShapeDtypeStruct((batch, 1, w), F32),
            jax.ShapeDtypeStruct((batch, LRU_CONV - 1, w), F32),
        ],
        scratch_shapes=[
            pltpu.VMEM((tl + SUBLANES, w), F32),
            pltpu.VMEM((1, w), F32),
            pltpu.VMEM((tl, w), F32),
            pltpu.VMEM((tl, w), F32),
            pltpu.VMEM((tl, w), F32),
        ],
        compiler_params=_params("parallel", "arbitrary"),
        name="lru_prompt",
    )(z, z, buf, h0, p['lru_cw'], p['lru_cb'], p['lru_wa'], p['lru_ba'], p['lru_wx'], p['lru_bx'], p['lru_lam'])
    return out_a, h_last[:, 0], nbuf


def _token_slab(refs, k, bb, t):
    return jnp.concatenate([ref[pl.ds(k, bb, stride=t), :] for ref in refs], axis=-1)


def _store_token_slab(stage_ref, k, bb, t, val):
    for c in range(stage_ref.shape[0]):
        stage_ref[c, pl.ds(k, bb, stride=t), :] = val[:, c * LANES:(c + 1) * LANES]


def _unstage(stage_ref, out_ref):
    for c in range(stage_ref.shape[0]):
        out_ref[:, c * LANES:(c + 1) * LANES] = stage_ref[c]


def _lane_tile_specs(rows, width, col0):
    nl = width // LANES
    return [pl.BlockSpec((rows, LANES), lambda i, c=c: (i, col0 * nl + c)) for c in range(nl)]


def _lru_sample_kernel(*refs, t, bb, nl):
    zx_refs, zg_refs = refs[:nl], refs[nl:2 * nl]
    (buf_ref, h0_ref, cw_ref, cb_ref, wa_ref, ba_ref, wx_ref, bx_ref, lam_ref,
     out_ref, hlast_ref, nbuf_ref, stage_ref) = refs[2 * nl:]
    xp = [buf_ref[k] for k in range(LRU_CONV - 1)] + [_token_slab(zx_refs, k, bb, t) for k in range(t)]
    h = h0_ref[...]
    for k in range(t):
        xc = cb_ref[...] + cw_ref[0:1, :] * xp[k]
        for j in range(1, LRU_CONV):
            xc = xc + cw_ref[j:j + 1, :] * xp[k + j]
        a, u = _lru_gates(xc, wa_ref, ba_ref, wx_ref, bx_ref, lam_ref)
        h = a * h + u
        _store_token_slab(stage_ref, k, bb, t, h * _silu(_token_slab(zg_refs, k, bb, t)))
    _unstage(stage_ref, out_ref)
    hlast_ref[...] = h
    for j in range(LRU_CONV - 1):
        nbuf_ref[j] = xp[t + j]


def _lru_sample(z, layer, buf_t, h0, p, batch, t):
    w = p['lru_cw'].shape[1]
    nl = w // LANES
    bb = _seq_block(batch, LRU_SAMPLE_SEQS)
    vec = pl.BlockSpec((1, w), lambda i: (0, 0))
    mat = pl.BlockSpec((w, w), lambda i: (0, 0))
    out_a, h_last, nbuf = pl.pallas_call(
        functools.partial(_lru_sample_kernel, t=t, bb=bb, nl=nl),
        grid=(batch // bb,),
        in_specs=[
            *_lane_tile_specs(bb * t, w, 0),
            *_lane_tile_specs(bb * t, w, 1),
            pl.BlockSpec((None, LRU_CONV - 1, bb, w), lambda i: (layer, 0, i, 0)),
            pl.BlockSpec((None, bb, w), lambda i: (layer, i, 0)),
            pl.BlockSpec((LRU_CONV, w), lambda i: (0, 0)),
            vec, mat, vec, mat, vec, vec,
        ],
        out_specs=[
            pl.BlockSpec((bb * t, w), lambda i: (i, 0)),
            pl.BlockSpec((bb, w), lambda i: (i, 0)),
            pl.BlockSpec((LRU_CONV - 1, bb, w), lambda i: (0, i, 0)),
        ],
        out_shape=[
            jax.ShapeDtypeStruct((batch * t, w), F32),
            jax.ShapeDtypeStruct((batch, w), F32),
            jax.ShapeDtypeStruct((LRU_CONV - 1, batch, w), F32),
        ],
        scratch_shapes=[pltpu.VMEM((nl, bb * t, LANES), F32)],
        compiler_params=_params("parallel"),
        name="lru_sample",
    )(*([z] * (2 * nl)), buf_t, h0, p['lru_cw'], p['lru_cb'], p['lru_wa'], p['lru_ba'], p['lru_wx'], p['lru_bx'],
      p['lru_lam'])
    return out_a, h_last, jnp.transpose(nbuf, (1, 0, 2))


CONV_HIST = CONV_K - 1
CONV_PAD = 32


def _ln_silu_gate(y, lng_ref, lnb_ref, gate):
    mu = jnp.mean(y, axis=-1, keepdims=True)
    yc = y - mu
    yn = yc * lax.rsqrt(jnp.mean(yc * yc, axis=-1, keepdims=True) + LN_EPS) * lng_ref[...] + lnb_ref[...]
    return _silu(yn) * _silu(gate)


def _conv_kernel(zv_ref, zglu_ref, zg_ref, buf_ref, w_ref, b_ref, lng_ref, lnb_ref,
                 out_ref, nbuf_ref, s_ref, win_ref, *, tl):
    base = CONV_PAD - CONV_HIST

    @pl.when(pl.program_id(1) == 0)
    def _():
        s_ref[base:CONV_PAD, :] = buf_ref[...]

    s_ref[CONV_PAD:CONV_PAD + tl, :] = zv_ref[...] * jax.nn.sigmoid(zglu_ref[...])
    y = b_ref[...]
    for shift in range(SUBLANES):
        taps = range(shift, CONV_K, SUBLANES)
        rows = taps[-1] - shift + tl
        win_ref[shift, 0:rows, :] = s_ref[base + shift:base + shift + rows, :]
        for k in taps:
            y = y + w_ref[k:k + 1, :] * win_ref[shift, k - shift:k - shift + tl, :]
    tail = s_ref[base + tl:CONV_PAD + tl, :]
    s_ref[base:CONV_PAD, :] = tail
    nbuf_ref[...] = tail
    out_ref[...] = _ln_silu_gate(y, lng_ref, lnb_ref, zg_ref[...])


def _conv_prompt(z, buf, p, batch, seq):
    cw = p['dw_w'].shape[1]
    col0 = p['conv_col0']
    tl = min(SEQ_TILE, seq)
    nc = seq // tl
    vec = pl.BlockSpec((1, cw), lambda b_, c: (0, 0))
    zspec = lambda off: pl.BlockSpec((tl, cw), lambda b_, c: (b_ * nc + c, col0 + off))
    return pl.pallas_call(
        functools.partial(_conv_kernel, tl=tl),
        grid=(batch, nc),
        in_specs=[
            zspec(0), zspec(1), zspec(2),
            pl.BlockSpec((None, CONV_HIST, cw), lambda b_, c: (b_, 0, 0)),
            pl.BlockSpec((CONV_K, cw), lambda b_, c: (0, 0)),
            vec, vec, vec,
        ],
        out_specs=[
            pl.BlockSpec((tl, cw), lambda b_, c: (b_ * nc + c, 0)),
            pl.BlockSpec((None, CONV_HIST, cw), lambda b_, c: (b_, 0, 0)),
        ],
        out_shape=[
            jax.ShapeDtypeStruct((batch * seq, cw), F32),
            jax.ShapeDtypeStruct((batch, CONV_HIST, cw), F32),
        ],
        scratch_shapes=[pltpu.VMEM((tl + CONV_PAD, cw), F32),
                        pltpu.VMEM((SUBLANES, tl + CONV_PAD - SUBLANES, cw), F32)],
        compiler_params=_params("parallel", "arbitrary"),
        name="conv_prompt",
    )(z, z, z, buf, p['dw_w'], p['dw_b'], p['ln_g'], p['ln_b'])


def _conv_sample_kernel(*refs, t, bb, nl):
    zv_refs, zglu_refs, zg_refs = refs[:nl], refs[nl:2 * nl], refs[2 * nl:3 * nl]
    buf_ref, w_ref, b_ref, lng_ref, lnb_ref, out_ref, nbuf_ref, u_ref, stage_ref = refs[3 * nl:]
    for k in range(t):
        u_ref[k] = _token_slab(zv_refs, k, bb, t) * jax.nn.sigmoid(_token_slab(zglu_refs, k, bb, t))

    def tap(j):
        return buf_ref[j] if j < CONV_HIST else u_ref[j - CONV_HIST]

    for k in range(t):
        y = b_ref[...] + w_ref[0:1, :] * tap(k)
        for j in range(1, CONV_K):
            y = y + w_ref[j:j + 1, :] * tap(k + j)
        _store_token_slab(stage_ref, k, bb, t, _ln_silu_gate(y, lng_ref, lnb_ref, _token_slab(zg_refs, k, bb, t)))
    _unstage(stage_ref, out_ref)
    for j in range(CONV_HIST):
        nbuf_ref[j] = tap(t + j)


def _conv_sample(z, layer, buf_t, p, batch, t):
    cw = p['dw_w'].shape[1]
    nl = cw // LANES
    col0 = p['conv_col0']
    bb = _seq_block(batch, CONV_SAMPLE_SEQS)
    vec = pl.BlockSpec((1, cw), lambda i: (0, 0))
    out_c, nbuf = pl.pallas_call(
        functools.partial(_conv_sample_kernel, t=t, bb=bb, nl=nl),
        grid=(batch // bb,),
        in_specs=[
            *_lane_tile_specs(bb * t, cw, col0),
            *_lane_tile_specs(bb * t, cw, col0 + 1),
            *_lane_tile_specs(bb * t, cw, col0 + 2),
            pl.BlockSpec((None, CONV_HIST, bb, cw), lambda i: (layer, 0, i, 0)),
            pl.BlockSpec((CONV_K, cw), lambda i: (0, 0)),
            vec, vec, vec,
        ],
        out_specs=[
            pl.BlockSpec((bb * t, cw), lambda i: (i, 0)),
            pl.BlockSpec((CONV_HIST, bb, cw), lambda i: (0, i, 0)),
        ],
        out_shape=[
            jax.ShapeDtypeStruct((batch * t, cw), F32),
            jax.ShapeDtypeStruct((CONV_HIST, batch, cw), F32),
        ],
        scratch_shapes=[pltpu.VMEM((t, bb, cw), F32), pltpu.VMEM((nl, bb * t, LANES), F32)],
        compiler_params=_params("parallel"),
        name="conv_sample",
    )(*([z] * (3 * nl)), buf_t, p['dw_w'], p['dw_b'], p['ln_g'], p['ln_b'])
    return out_c, jnp.transpose(nbuf, (1, 0, 2))


def _rope128(v, table):
    a = v * table
    return a + pltpu.roll(a, QK_ROPE, 1)


def _mla_proj_kernel(zq_ref, zkv_ref, zkr_ref, qn_ref, kvn_ref, wn_ref, wr_ref, wuk_ref, tab_ref,
                     qa_ref, qr_ref, ckv_ref, ckvb_ref, krope_ref, krp_ref):
    tab = tab_ref[...]
    cq = _rms(zq_ref[...], qn_ref[...]).astype(BF16)
    q_nope = _dot(cq, wn_ref[...])
    q_rope = _dot(cq, wr_ref[...])
    for h in range(MLA_HEADS):
        qh = q_nope[:, h * QK_NOPE:(h + 1) * QK_NOPE].astype(BF16)
        qa_ref[h] = _dot(qh, wuk_ref[h]).astype(qa_ref.dtype)
        qr_ref[h] = _rope128(q_rope[:, h * LANES:(h + 1) * LANES], tab).astype(qr_ref.dtype)
    ckv = _rms(zkv_ref[...], kvn_ref[...])
    ckv_ref[...] = ckv
    ckvb_ref[...] = ckv.astype(BF16)
    kro = _rope128(zkr_ref[...], tab)
    krope_ref[...] = kro[:, :QK_ROPE]
    lane = lax.broadcasted_iota(jnp.int32, kro.shape, 1)
    krp_ref[...] = jnp.where(lane < QK_ROPE, kro, 0.0).astype(BF16)


def _mla_proj(z, zkr, p, table, q_dtype):
    n = z.shape[0]
    r = p['qn'].shape[1]
    tm = min(ROW_TILE, n)
    nt = table.shape[0] // tm
    wn, wr, wuk = p['wn'], p['wr'], p['wuk']
    full = lambda shape: pl.BlockSpec(shape, lambda i: (0,) * len(shape))
    return pl.pallas_call(
        _mla_proj_kernel,
        grid=(n // tm,),
        in_specs=[
            pl.BlockSpec((tm, r), lambda i: (i, 2)),
            pl.BlockSpec((tm, r), lambda i: (i, 3)),
            pl.BlockSpec((tm, LANES), lambda i: (i, 0)),
            full((1, r)), full((1, r)),
            full(wn.shape), full(wr.shape), full(wuk.shape),
            pl.BlockSpec((tm, LANES), lambda i: (i % nt, 0)),
        ],
        out_specs=[
            pl.BlockSpec((MLA_HEADS, tm, r), lambda i: (0, i, 0)),
            pl.BlockSpec((MLA_HEADS, tm, LANES), lambda i: (0, i, 0)),
            pl.BlockSpec((tm, r), lambda i: (i, 0)),
            pl.BlockSpec((tm, r), lambda i: (i, 0)),
            pl.BlockSpec((tm, QK_ROPE), lambda i: (i, 0)),
            pl.BlockSpec((tm, LANES), lambda i: (i, 0)),
        ],
        out_shape=[
            jax.ShapeDtypeStruct((MLA_HEADS, n, r), q_dtype),
            jax.ShapeDtypeStruct((MLA_HEADS, n, LANES), q_dtype),
            jax.ShapeDtypeStruct((n, r), F32),
            jax.ShapeDtypeStruct((n, r), BF16),
            jax.ShapeDtypeStruct((n, QK_ROPE), F32),
            jax.ShapeDtypeStruct((n, LANES), BF16),
        ],
        compiler_params=_params("parallel"),
        name="mla_proj",
    )(z, z, zkr, p['qn'], p['kvn'], wn, wr, wuk, table)


def _local_softmax(s, v):
    m = jnp.max(s, axis=-1, keepdims=True)
    p = jnp.exp(s - m)
    return m, jnp.sum(p, axis=-1, keepdims=True), _dot(p.astype(BF16), v)


def _merge_softmax(parts):
    m = parts[0][0]
    for part in parts[1:]:
        m = jnp.maximum(m, part[0])
    l = acc = None
    for m_g, l_g, acc_g in parts:
        w = jnp.exp(m_g - m)
        l = l_g * w if l is None else l + l_g * w
        acc = acc_g * w if acc is None else acc + acc_g * w
    return m, l, acc


def _prompt_attn_kernel(qi_ref, kj_ref, qa_ref, qr_ref, kc_ref, kr_ref, v_ref, wuv_ref, zg_ref,
                        out_ref, s_ref, m_ref, l_ref, acc_ref, *, tq, tk, n_tiles):
    n = pl.program_id(1)
    rows = MLA_HEADS * tq

    def scores():
        qa = qa_ref[...].reshape(rows, qa_ref.shape[-1])
        qr = qr_ref[...].reshape(rows, LANES)
        return (_dot_nt(qa, kc_ref[...]) + _dot_nt(qr, kr_ref[...])) * ATTN_SCALE

    prev = jnp.maximum(n - 1, 0)
    i = qi_ref[prev]
    j = kj_ref[prev]

    def absorb():
        q_pos = i * tq + (lax.broadcasted_iota(jnp.int32, (rows, 1), 0) & (tq - 1))
        k_pos = j * tk + lax.broadcasted_iota(jnp.int32, (1, tk), 1)
        s = jnp.where(k_pos <= q_pos, s_ref[...], -jnp.inf)
        restart = j == 0
        m_old = jnp.where(restart, -jnp.inf, m_ref[...])
        m_new = jnp.maximum(m_old, jnp.max(s, axis=-1, keepdims=True))
        corr = jnp.exp(m_old - m_new)
        p = jnp.exp(s - m_new)
        l_ref[...] = jnp.where(restart, 0.0, l_ref[...]) * corr + jnp.sum(p, axis=-1, keepdims=True)
        acc_ref[...] = jnp.where(restart, 0.0, acc_ref[...]) * corr + _dot(p.astype(BF16), v_ref[...])
        m_ref[...] = m_new

    @pl.when(n == 0)
    def _():
        m_ref[...] = jnp.full(m_ref.shape, -jnp.inf, F32)
        l_ref[...] = jnp.zeros(l_ref.shape, F32)
        acc_ref[...] = jnp.zeros(acc_ref.shape, F32)
        s_ref[...] = scores()

    @pl.when(jnp.logical_and(n > 0, n < n_tiles))
    def _():
        s_next = scores()
        absorb()
        s_ref[...] = s_next

    @pl.when(n == n_tiles)
    def _():
        absorb()

    @pl.when(jnp.logical_and(n > 0, j == (i * tq + tq - 1) // tk))
    def _():
        ob = (acc_ref[...] / l_ref[...]).astype(BF16)
        heads = [_dot(ob[h * tq:(h + 1) * tq], wuv_ref[h]) for h in range(MLA_HEADS)]
        out_ref[...] = jnp.concatenate(heads, axis=-1) * _silu(zg_ref[...])


def _causal_tiles(seq, tq, tk):
    tiles = [(i, j) for i in range(seq // tq) for j in range((i * tq + tq - 1) // tk + 1)]
    tiles.append(tiles[-1])
    return jnp.array([t[0] for t in tiles], jnp.int32), jnp.array([t[1] for t in tiles], jnp.int32), len(tiles) - 1


def _prompt_attention(qa, qr, kc, kr, wuv, z, batch, seq):
    r = qa.shape[-1]
    tq = min(Q_TILE, seq)
    tk = min(KV_TILE, seq)
    assert tq & (tq - 1) == 0
    nq, nk = seq // tq, seq // tk
    width = wuv.shape[0] * wuv.shape[2]
    qi, kj, n_tiles = _causal_tiles(seq, tq, tk)
    lag = lambda n: jnp.maximum(n - 1, 0)
    grid_spec = pltpu.PrefetchScalarGridSpec(
        num_scalar_prefetch=2,
        grid=(batch, n_tiles + 1),
        in_specs=[
            pl.BlockSpec((MLA_HEADS, tq, r), lambda b, n, qi, kj: (0, b * nq + qi[n], 0)),
            pl.BlockSpec((MLA_HEADS, tq, LANES), lambda b, n, qi, kj: (0, b * nq + qi[n], 0)),
            pl.BlockSpec((tk, r), lambda b, n, qi, kj: (b * nk + kj[n], 0)),
            pl.BlockSpec((tk, LANES), lambda b, n, qi, kj: (b * nk + kj[n], 0)),
            pl.BlockSpec((tk, r), lambda b, n, qi, kj: (b * nk + kj[lag(n)], 0)),
            pl.BlockSpec(wuv.shape, lambda b, n, qi, kj: (0, 0, 0)),
            pl.BlockSpec((tq, width), lambda b, n, qi, kj: (b * nq + qi[lag(n)], 2)),
        ],
        out_specs=pl.BlockSpec((tq, width), lambda b, n, qi, kj: (b * nq + qi[lag(n)], 0)),
        scratch_shapes=[
            pltpu.VMEM((MLA_HEADS * tq, tk), F32),
            pltpu.VMEM((MLA_HEADS * tq, 1), F32),
            pltpu.VMEM((MLA_HEADS * tq, 1), F32),
            pltpu.VMEM((MLA_HEADS * tq, r), F32),
        ],
    )
    return pl.pallas_call(
        functools.partial(_prompt_attn_kernel, tq=tq, tk=tk, n_tiles=n_tiles),
        grid_spec=grid_spec,
        out_shape=jax.ShapeDtypeStruct((batch * seq, width), F32),
        compiler_params=_params("parallel", "arbitrary"),
        name="prompt_attention",
    )(qi, kj, qa, qr, kc, kr, kc, wuv, z)


def _sample_attn_kernel(pt_ref, qa_ref, qr_ref, kcn_ref, krn_ref, wuv_ref, zg_ref, lat_hbm, rope_hbm,
                        out_ref, lat_buf, rope_buf, sem, kb_buf, m_ref, l_ref, acc_ref,
                        *, layer, t, n_pages, chunk, group, seqs):
    b = pl.program_id(0)
    n_chunks = n_pages // chunk
    total = pl.num_programs(0) * seqs * n_chunks
    first = b * seqs * n_chunks
    rows = MLA_HEADS * t
    r = qa_ref.shape[-1]

    def page_copies(n):
        slot = n % PAGE_SLOTS
        copies = []
        for g in range(chunk):
            page = pt_ref[n * chunk + g]
            copies.append(pltpu.make_async_copy(lat_hbm.at[layer, page], lat_buf.at[slot, g], sem.at[0, slot]))
            copies.append(pltpu.make_async_copy(rope_hbm.at[layer, page], rope_buf.at[slot, g], sem.at[1, slot]))
        return copies

    def fetch(n):
        @pl.when(n < total)
        def _():
            for cp in page_copies(n):
                cp.start()

    def consume(n):
        fetch(n + PAGE_SLOTS - 1)
        for cp in page_copies(n):
            cp.wait()

    @pl.when(b == 0)
    def _():
        for n in range(PAGE_SLOTS - 1):
            fetch(n)

    qa = [qa_ref[:, k * t:(k + 1) * t, :].reshape(rows, r).astype(BF16) for k in range(seqs)]
    qr = [qr_ref[:, k * t:(k + 1) * t, :].reshape(rows, LANES)[:, :QK_ROPE].astype(BF16) for k in range(seqs)]

    def scores(n, qa_k, qr_k, kb_slot):
        slot = n % PAGE_SLOTS
        kb = lat_buf[slot].reshape(chunk * PAGE_SIZE, r).astype(BF16)
        kb_buf[kb_slot] = kb
        s_rope = jnp.concatenate([_dot(qr_k, rope_buf[slot, g].astype(BF16)) for g in range(chunk)], axis=-1)
        return (_dot_nt(qa_k, kb) + s_rope) * ATTN_SCALE

    def absorb(s, kb_slot):
        parts = [(m_ref[...], l_ref[...], acc_ref[...])]
        for k0 in range(0, chunk * PAGE_SIZE, group * PAGE_SIZE):
            k1 = k0 + group * PAGE_SIZE
            parts.append(_local_softmax(s[:, k0:k1], kb_buf[kb_slot, k0:k1, :]))
        m, l, acc = _merge_softmax(parts)
        m_ref[...] = m
        l_ref[...] = l
        acc_ref[...] = acc

    def start_state(k):
        tok = slice(k * t, (k + 1) * t)
        kc = jnp.concatenate([kcn_ref[tok, :], jnp.zeros((LANES - t, r), F32)], axis=0).astype(BF16)
        kr = jnp.concatenate([krn_ref[tok, :], jnp.zeros((LANES - t, QK_ROPE), F32)], axis=0).astype(BF16)
        s_tok = (_dot_nt(qa[k], kc) + _dot_nt(qr[k], kr)) * ATTN_SCALE
        q_tok = lax.broadcasted_iota(jnp.int32, (rows, 1), 0) % t
        k_tok = lax.broadcasted_iota(jnp.int32, (1, LANES), 1)
        m0, l0, acc0 = _local_softmax(jnp.where(k_tok <= q_tok, s_tok, -jnp.inf), kc)
        m_ref[...] = m0
        l_ref[...] = l0
        acc_ref[...] = acc0

    def finish(k):
        tok = slice(k * t, (k + 1) * t)
        ob = (acc_ref[...] / l_ref[...]).astype(BF16)
        heads = [_dot(ob, wuv_ref[h])[h * t:(h + 1) * t] for h in range(MLA_HEADS)]
        out_ref[tok, :] = jnp.concatenate(heads, axis=-1) * _silu(zg_ref[tok, :])

    stream = [k for k in range(seqs) for _ in range(n_chunks)]
    consume(first)
    s = scores(first, qa[0], qr[0], 0)
    start_state(0)
    for i in range(1, len(stream)):
        k, k_prev = stream[i], stream[i - 1]
        consume(first + i)
        s_next = scores(first + i, qa[k], qr[k], i % 2)
        absorb(s, (i - 1) % 2)
        if k != k_prev:
            finish(k_prev)
            start_state(k)
        s = s_next
    absorb(s, (len(stream) - 1) % 2)
    finish(seqs - 1)


def _sample_attention(qa, qr, ckv, krope, cache_lat, cache_rope_t, layer, page_table, wuv, z, batch, t):
    assert t == SUBLANES
    r = qa.shape[-1]
    n_pages = page_table.shape[1]
    chunk = min(PAGES_PER_CHUNK, n_pages)
    group = min(PAGES_PER_GROUP, chunk)
    assert n_pages % chunk == 0 and chunk % group == 0
    width = wuv.shape[0] * wuv.shape[2]
    seqs = SEQS_PER_STEP if batch % SEQS_PER_STEP == 0 else 1
    grid_spec = pltpu.PrefetchScalarGridSpec(
        num_scalar_prefetch=1,
        grid=(batch // seqs,),
        in_specs=[
            pl.BlockSpec((MLA_HEADS, seqs * t, r), lambda b, pt_ref: (0, b, 0)),
            pl.BlockSpec((MLA_HEADS, seqs * t, LANES), lambda b, pt_ref: (0, b, 0)),
            pl.BlockSpec((seqs * t, r), lambda b, pt_ref: (b, 0)),
            pl.BlockSpec((seqs * t, QK_ROPE), lambda b, pt_ref: (b, 0)),
            pl.BlockSpec(wuv.shape, lambda b, pt_ref: (0, 0, 0)),
            pl.BlockSpec((seqs * t, width), lambda b, pt_ref: (b, 2)),
            pl.BlockSpec(memory_space=pl.ANY),
            pl.BlockSpec(memory_space=pl.ANY),
        ],
        out_specs=pl.BlockSpec((seqs * t, width), lambda b, pt_ref: (b, 0)),
        scratch_shapes=[
            pltpu.VMEM((PAGE_SLOTS, chunk, PAGE_SIZE, r), F32),
            pltpu.VMEM((PAGE_SLOTS, chunk, QK_ROPE, PAGE_SIZE), F32),
            pltpu.SemaphoreType.DMA((2, PAGE_SLOTS)),
            pltpu.VMEM((2, chunk * PAGE_SIZE, r), BF16),
            pltpu.VMEM((MLA_HEADS * t, 1), F32),
            pltpu.VMEM((MLA_HEADS * t, 1), F32),
            pltpu.VMEM((MLA_HEADS * t, r), F32),
        ],
    )
    return pl.pallas_call(
        functools.partial(_sample_attn_kernel, layer=layer, t=t, n_pages=n_pages, chunk=chunk, group=group,
                          seqs=seqs),
        grid_spec=grid_spec,
        out_shape=jax.ShapeDtypeStruct((batch * t, width), F32),
        compiler_params=_params("arbitrary"),
        name="sample_attention",
    )(page_table.reshape(-1), qa, qr, ckv, krope, wuv, z, cache_lat, cache_rope_t)


def _out_proj_kernel(a_ref, b_ref, c_ref, x_ref, w_ref, g_ref, y_ref, *, final_norm):
    wa, wb = a_ref.shape[1], b_ref.shape[1]
    y = x_ref[...] + _dot(a_ref[...].astype(BF16), w_ref[0:wa, :])
    y = y + _dot(b_ref[...].astype(BF16), w_ref[wa:wa + wb, :])
    y = y + _dot(c_ref[...].astype(BF16), w_ref[wa + wb:, :])
    if final_norm:
        y = _rms(y, g_ref[...])
    y_ref[...] = y


def _out_proj(out_a, out_b, out_c, x, w, g, final_norm):
    n, d = x.shape
    assert out_a.shape[1] + out_b.shape[1] + out_c.shape[1] == w.shape[0]
    tm = min(ROW_TILE // 2, n)
    rows = lambda width: pl.BlockSpec((tm, width), lambda i: (i, 0))
    full = lambda arr: pl.BlockSpec(arr.shape, lambda i: (0, 0))
    return pl.pallas_call(
        functools.partial(_out_proj_kernel, final_norm=final_norm),
        grid=(n // tm,),
        in_specs=[rows(out_a.shape[1]), rows(out_b.shape[1]), rows(out_c.shape[1]), rows(d), full(w), full(g)],
        out_specs=rows(d),
        out_shape=jax.ShapeDtypeStruct((n, d), F32),
        compiler_params=_params("parallel"),
        name="out_proj",
    )(out_a, out_b, out_c, x, w, g)


def _rope_table(pos, rows):
    half = QK_ROPE // 2
    freqs = ROPE_THETA ** (-jnp.arange(half, dtype=F32) / half)
    ang = pos.astype(F32)[:, None] * freqs[None, :]
    cos, sin = jnp.cos(ang), jnp.sin(ang)
    table = jnp.concatenate([cos, cos, -sin, sin], axis=-1)
    reps = max(1, rows // table.shape[0])
    return jnp.tile(table, (reps, 1))


def _block_diag(w):
    h, d, _ = w.shape
    eye = jnp.eye(h, dtype=w.dtype)
    return (eye[:, None, :, None] * w[:, :, None, :]).reshape(h * d, h * d)


def _swap_halves(w):
    half = w.shape[-1] // 2
    return jnp.concatenate([w[..., half:], w[..., :half]], axis=-1)


def _layer_weights(l, norm_g, w_in, w_out, lru_conv_w, lru_conv_b, lru_w_a, lru_b_a, lru_w_x, lru_b_x, lru_lambda,
                   mla_q_norm, mla_kv_norm, mla_w_uq, mla_w_uk, mla_w_uv, conv_dw_w, conv_dw_b, conv_ln_g,
                   conv_ln_b):
    lw = lru_conv_w.shape[2]
    q_rank = mla_q_norm.shape[1]
    kv_rank = mla_kv_norm.shape[1]
    mla_w = mla_w_uv.shape[2] * mla_w_uv.shape[3]
    cw = conv_dw_w.shape[2]
    sizes = (lw, lw, q_rank, kv_rank, QK_ROPE, mla_w, cw, cw, cw)
    offs = [0]
    for s in sizes:
        offs.append(offs[-1] + s)
    w = w_in[l]
    col = lambda k: w[:, offs[k]:offs[k + 1]]
    w_main = jnp.concatenate([col(0), col(1), col(2), col(3), col(5), col(6), col(7), col(8)], axis=1).astype(BF16)
    w_kr = jnp.concatenate([col(4), _swap_halves(col(4))], axis=1).astype(BF16)
    wuq = mla_w_uq[l]
    wn = wuq[:, :, :QK_NOPE].reshape(q_rank, -1).astype(BF16)
    wrope = wuq[:, :, QK_NOPE:]
    wr = jnp.concatenate([wrope, _swap_halves(wrope)], axis=-1).reshape(q_rank, -1).astype(BF16)
    wuk = jnp.transpose(mla_w_uk[l], (1, 2, 0)).astype(BF16)
    wuv = jnp.transpose(mla_w_uv[l], (1, 0, 2)).astype(BF16)
    wo = w_out[l].astype(BF16)
    row = lambda v: v[l][None, :]
    return dict(
        norm_g=row(norm_g), w_main=w_main, w_kr=w_kr,
        lru_cw=lru_conv_w[l], lru_cb=row(lru_conv_b),
        lru_wa=_block_diag(lru_w_a[l]).astype(BF16), lru_ba=row(lru_b_a),
        lru_wx=_block_diag(lru_w_x[l]).astype(BF16), lru_bx=row(lru_b_x), lru_lam=row(lru_lambda),
        qn=row(mla_q_norm), kvn=row(mla_kv_norm), wn=wn, wr=wr, wuk=wuk, wuv=wuv,
        dw_w=conv_dw_w[l], dw_b=row(conv_dw_b), ln_g=row(conv_ln_g), ln_b=row(conv_ln_b),
        wo=wo,
        conv_col0=(2 * lw + q_rank + kv_rank + mla_w) // cw,
    )


def _mixer_layer(x, p, table, lru_fn, conv_fn, attend, q_dtype, final_g):
    z, zkr = _in_proj(x, p['norm_g'], p['w_main'], p['w_kr'])
    out_a, h_last, lru_buf_new = lru_fn(z, p)
    out_c, conv_buf_new = conv_fn(z, p)
    qa, qr, ckv, ckv_bf, krope, krp = _mla_proj(z, zkr, p, table, q_dtype)
    out_b = attend(qa, qr, ckv, ckv_bf, krope, krp, p['wuv'], z)
    y = _out_proj(out_a, out_b, out_c, x, p['wo'],
                  final_g if final_g is not None else p['norm_g'], final_g is not None)
    return y, (ckv, krope, h_last, lru_buf_new, conv_buf_new)


def kernel(x_prompt, x_sample, cache_kv_latent, cache_k_rope, page_table, state_lru_h, state_lru_conv, state_conv, norm_g, w_in, w_out, lru_conv_w, lru_conv_b, lru_w_a, lru_b_a, lru_w_x, lru_b_x, lru_lambda, mla_q_norm, mla_kv_norm, mla_w_uq, mla_w_uk, mla_w_uv, conv_dw_w, conv_dw_b, conv_ln_g, conv_ln_b, final_norm_g):
    b_p, seq, d_model = x_prompt.shape
    b_s, dec_seq, _ = x_sample.shape
    depth = norm_g.shape[0]
    lw = lru_conv_w.shape[2]
    cw = conv_dw_w.shape[2]
    past_len = page_table.shape[1] * PAGE_SIZE
    n_p, n_s = b_p * seq, b_s * dec_seq
    table_p = _rope_table(jnp.arange(seq, dtype=F32), min(ROW_TILE, n_p))
    table_s = _rope_table(past_len + jnp.arange(dec_seq, dtype=F32), min(ROW_TILE, n_s))
    zero_h = jnp.zeros((b_p, 1, lw), F32)
    zero_lbuf = jnp.zeros((b_p, LRU_CONV - 1, lw), F32)
    zero_cbuf = jnp.zeros((b_p, CONV_K - 1, cw), F32)
    lru_conv_t = jnp.transpose(state_lru_conv, (0, 2, 1, 3))
    conv_t = jnp.transpose(state_conv, (0, 2, 1, 3))
    cache_rope_t = jnp.swapaxes(cache_k_rope, 2, 3)
    final_g = final_norm_g[None, :]

    xp = x_prompt.reshape(n_p, d_model)
    xs = x_sample.reshape(n_s, d_model)
    st_p, st_s = [], []
    for l in range(depth):
        p = _layer_weights(l, norm_g, w_in, w_out, lru_conv_w, lru_conv_b, lru_w_a, lru_b_a, lru_w_x, lru_b_x,
                           lru_lambda, mla_q_norm, mla_kv_norm, mla_w_uq, mla_w_uk, mla_w_uv, conv_dw_w,
                           conv_dw_b, conv_ln_g, conv_ln_b)
        last = final_g if l == depth - 1 else None

        def attend_p(qa, qr, ckv, ckv_bf, krope, krp, wuv, z):
            return _prompt_attention(qa, qr, ckv_bf, krp, wuv, z, b_p, seq)

        def attend_s(qa, qr, ckv, ckv_bf, krope, krp, wuv, z, l=l):
            return _sample_attention(qa, qr, ckv, krope, cache_kv_latent, cache_rope_t, l, page_table, wuv, z,
                                     b_s, dec_seq)

        xp, sp = _mixer_layer(
            xp, p, table_p,
            lambda z, p: _lru_prompt(z, zero_lbuf, zero_h, p, b_p, seq),
            lambda z, p: _conv_prompt(z, zero_cbuf, p, b_p, seq),
            attend_p, BF16, last)
        xs, ss = _mixer_layer(
            xs, p, table_s,
            lambda z, p, l=l: _lru_sample(z, l, lru_conv_t, state_lru_h, p, b_s, dec_seq),
            lambda z, p, l=l: _conv_sample(z, l, conv_t, p, b_s, dec_seq),
            attend_s, F32, last)
        st_p.append(sp)
        st_s.append(ss)

    def stack(states, k, shape):
        return jnp.stack([s[k].reshape(shape) for s in states])

    r = mla_kv_norm.shape[1]
    return (xp.reshape(b_p, seq, d_model), xs.reshape(b_s, dec_seq, d_model),
            stack(st_p, 0, (b_p, seq, r)), stack(st_p, 1, (b_p, seq, QK_ROPE)), stack(st_p, 2, (b_p, lw)),
            stack(st_p, 3, (b_p, LRU_CONV - 1, lw)), stack(st_p, 4, (b_p, CONV_K - 1, cw)),
            stack(st_s, 0, (b_s, dec_seq, r)), stack(st_s, 1, (b_s, dec_seq, QK_ROPE)), stack(st_s, 2, (b_s, lw)),
            stack(st_s, 3, (b_s, LRU_CONV - 1, lw)), stack(st_s, 4, (b_s, CONV_K - 1, cw)))
```
